```python
import math
import jax
import jax.numpy as jnp
from jax import lax
import numpy as np

D_MODEL = 1024
BATCH = 32
SEQ = 256
DEPTH = 4
DEC_BATCH = 4
DEC_SEQ = 2048
PAST_LEN = 512

GRID_W = 64
N_MIXERS = 4
N_ATTN_LAYERS = (DEPTH + 3) // N_MIXERS
N_HGRN_LAYERS = (DEPTH + 2) // N_MIXERS
N_CONV_LAYERS = (DEPTH + 1) // N_MIXERS
N_SCONV_LAYERS = DEPTH // N_MIXERS
ATTN_HEAD_DIM = 64
ATTN_HEADS = D_MODEL // (2 * ATTN_HEAD_DIM)
ROPE_BASE = 10000.0
Q_BLOCK = 128
HGRN_DK = 128
HGRN_HEADS = D_MODEL // HGRN_DK
HGRN_DV = D_MODEL // HGRN_HEADS
CHUNK = 32
CONV_WIDTH = 31
SHORT_CONV_WIDTH = 3
D_FF = ((8 * D_MODEL // 3 + 127) // 128) * 128
EPS = 1e-6

kernel_name = 'hybrid_diffusion_interleaved_step'


def rmsnorm(x, g):
    xf = x.astype(jnp.float32)
    y = xf * lax.rsqrt(jnp.mean(xf * xf, axis=-1, keepdims=True) + EPS)
    return (y * g.astype(jnp.float32)).astype(x.dtype)


def layernorm(x, g, b):
    xf = x.astype(jnp.float32)
    mu = jnp.mean(xf, axis=-1, keepdims=True)
    xc = xf - mu
    y = xc * lax.rsqrt(jnp.mean(xc * xc, axis=-1, keepdims=True) + EPS)
    return (y * g.astype(jnp.float32) + b.astype(jnp.float32)).astype(x.dtype)


def modulation(cvec, w, b):
    m = jax.nn.silu(cvec) @ w + b
    return m.reshape(cvec.shape[0], 3, 3, D_MODEL)


def pre_norm(x, mod, s, g):
    return rmsnorm(x, g) * (1.0 + mod[:, s, 1][:, None]) + mod[:, s, 0][:, None]


def post_add(x, out, mod, s, g, weight):
    return x + weight * mod[:, s, 2][:, None] * rmsnorm(out, g)


def swiglu(h, w_in, w_out):
    gate, up = jnp.split(h @ w_in, 2, axis=-1)
    return (jax.nn.silu(gate) * up) @ w_out


def depthwise_conv(x, w):
    pad = w.shape[0] // 2
    return lax.conv_general_dilated(
        x, w[:, None, :].astype(x.dtype), window_strides=(1,), padding=[(pad, pad)],
        dimension_numbers=('NWC', 'WIO', 'NWC'), feature_group_count=x.shape[-1])


def axial_rope_tables(n_tokens):
    rows = n_tokens // GRID_W
    row = jnp.repeat(jnp.arange(rows), GRID_W).astype(jnp.float32)
    col = jnp.tile(jnp.arange(GRID_W), rows).astype(jnp.float32)
    n_freq = ATTN_HEAD_DIM // 4
    inv_freq = ROPE_BASE ** (-jnp.arange(n_freq, dtype=jnp.float32) / n_freq)
    ang_r = row[:, None] * inv_freq
    ang_c = col[:, None] * inv_freq
    ang = jnp.concatenate([ang_r, ang_r, ang_c, ang_c], axis=-1)
    return jnp.cos(ang), jnp.sin(ang)


def apply_axial_rope(x, cos, sin):
    xs = x.reshape(x.shape[:-1] + (2, 2, ATTN_HEAD_DIM // 4))
    rot = jnp.stack([-xs[..., 1, :], xs[..., 0, :]], axis=-2).reshape(x.shape)
    c = cos[:, None, None].astype(x.dtype)
    s = sin[:, None, None].astype(x.dtype)
    return x * c + rot * s


def diff_softmax_attention(q, k, v, lam):
    B, T, H, _, HD = q.shape
    qb = q.reshape(B, T // Q_BLOCK, Q_BLOCK, H, 2, HD).swapaxes(0, 1)
    scale = HD ** -0.5

    def block(qi):
        s = jnp.einsum('bqhcd,bkhcd->bhcqk', qi, k).astype(jnp.float32) * scale
        p = jax.nn.softmax(s, axis=-1)
        w = p[:, :, 0] - lam * p[:, :, 1]
        return jnp.einsum('bhqk,bkhe->bqhe', w.astype(v.dtype), v)

    o = lax.map(block, qb)
    return o.swapaxes(0, 1).reshape(B, T, H, v.shape[-1])


def diff_attention(h, ctx_k, ctx_v, rope, w_qkv, w_o, lam_params, subln_g, layer_idx):
    B, T, _ = h.shape
    q, k, v = jnp.split(h @ w_qkv, 3, axis=-1)
    q = q.reshape(B, T, ATTN_HEADS, 2, ATTN_HEAD_DIM)
    k = k.reshape(B, T, ATTN_HEADS, 2, ATTN_HEAD_DIM)
    v = v.reshape(B, T, ATTN_HEADS, 2 * ATTN_HEAD_DIM)
    if rope is not None:
        q = apply_axial_rope(q, rope[0], rope[1])
        k = apply_axial_rope(k, rope[0], rope[1])
    k_flat = k.reshape(B, T, ATTN_HEADS, 2 * ATTN_HEAD_DIM)
    if ctx_k is None:
        keys, vals = k_flat, v
    else:
        keys = jnp.concatenate([ctx_k, k_flat], axis=1)
        vals = jnp.concatenate([ctx_v, v], axis=1)
    lam_init = 0.8 - 0.6 * math.exp(-0.3 * layer_idx)
    lp = lam_params.astype(jnp.float32)
    lam = jnp.exp(jnp.sum(lp[0] * lp[1])) - jnp.exp(jnp.sum(lp[2] * lp[3])) + lam_init
    o = diff_softmax_attention(q, keys.reshape(B, -1, ATTN_HEADS, 2, ATTN_HEAD_DIM), vals, lam)
    o = rmsnorm(o, subln_g) * (1.0 - lam_init)
    return o.reshape(B, T, D_MODEL) @ w_o, k_flat, v


def gla_chunked(q, k, v, logf, s0):
    B, T, H, DK = q.shape
    DV = v.shape[-1]
    n = T // CHUNK

    def to_chunks(t):
        return t.astype(jnp.float32).reshape(B, n, CHUNK, H, t.shape[-1]).swapaxes(0, 1)

    causal = jnp.tril(jnp.ones((CHUNK, CHUNK), bool))[:, :, None, None]

    def step(S, xs):
        qc, kc, vc, lf = xs
        b = jnp.cumsum(lf, axis=1)
        decay = jnp.exp(jnp.where(causal, b[:, :, None] - b[:, None, :], -jnp.inf))
        scores = jnp.einsum('bthd,bshd,btshd->bhts', qc, kc, decay)
        o = (jnp.einsum('bhts,bshv->bthv', scores, vc)
             + jnp.einsum('bthd,bhdv->bthv', qc * jnp.exp(b), S))
        b_last = b[:, -1]
        S_new = (jnp.exp(b_last)[..., None] * S
                 + jnp.einsum('bshd,bshv->bhdv', kc * jnp.exp(b_last[:, None] - b), vc))
        return S_new, o

    s_final, o = lax.scan(step, s0.astype(jnp.float32), (to_chunks(q), to_chunks(k), to_chunks(v), to_chunks(logf)))
    return o.swapaxes(0, 1).reshape(B, T, H, DV), s_final


def hgrn2_mixer(h, s0_fwd, s0_bwd, w_in, w_o, norm_g, lb):
    B, T, _ = h.shape
    shp = (B, T, HGRN_HEADS, HGRN_DK)
    q, zf, zb, i, g = jnp.split(h @ w_in, 5, axis=-1)
    q = q.reshape(shp)
    i = i.reshape(B, T, HGRN_HEADS, HGRN_DV)

    def gates(z, lbd):
        z = z.astype(jnp.float32).reshape(shp)
        lbd = lbd.astype(jnp.float32).reshape(HGRN_HEADS, HGRN_DK)
        log_f = jnp.logaddexp(jnp.log(lbd), jnp.log1p(-lbd) + jax.nn.log_sigmoid(z))
        k = (1.0 - lbd) * jax.nn.sigmoid(-z)
        return log_f, k

    logf_f, k_f = gates(zf, lb[0])
    logf_b, k_b = gates(zb, lb[1])
    o_f, s_f = gla_chunked(q, k_f, i, logf_f, s0_fwd)
    flip = lambda t: jnp.flip(t, axis=1)
    o_b, s_b = gla_chunked(flip(q), flip(k_b), flip(i), flip(logf_b), s0_bwd)
    o = o_f + flip(o_b)
    o = rmsnorm(o, norm_g) * jax.nn.silu(g.astype(jnp.float32).reshape(B, T, HGRN_HEADS, HGRN_DV))
    out = o.astype(h.dtype).reshape(B, T, D_MODEL) @ w_o
    return out, jnp.stack([s_f, s_b], axis=1).astype(h.dtype)


def conformer_conv(h, w_in, b_in, w_dw, b_dw, ln_g, ln_b, w_out, b_out):
    a, gt = jnp.split(h @ w_in + b_in, 2, axis=-1)
    u = a * jax.nn.sigmoid(gt)
    u = depthwise_conv(u, w_dw) + b_dw
    u = jax.nn.silu(layernorm(u, ln_g, ln_b))
    return u @ w_out + b_out


def short_gated_conv(h, w_in, w_conv, w_out):
    bg, cg, xin = jnp.split(h @ w_in, 3, axis=-1)
    u = depthwise_conv(cg * xin, w_conv)
    return (bg * u) @ w_out


def setup_inputs(seed: int = 0) -> dict:
    key = jax.random.key(seed)
    ks = iter(jax.random.split(key, 40))

    def nrm(shape, scale):
        return scale * jax.random.normal(next(ks), shape, jnp.float32)

    def gain(shape):
        return 1.0 + nrm(shape, 0.05)

    D = D_MODEL
    HV = 2 * ATTN_HEAD_DIM
    return {
        'x_prompt': nrm((BATCH, SEQ, D), 1.0),
        'x_sample': nrm((DEC_BATCH, DEC_SEQ, D), 1.0),
        'c': nrm((DEC_BATCH, D), 1.0),
        'cache_k': nrm((DEC_BATCH, N_ATTN_LAYERS, PAST_LEN, ATTN_HEADS, HV), 1.0),
        'cache_v': nrm((DEC_BATCH, N_ATTN_LAYERS, PAST_LEN, ATTN_HEADS, HV), 1.0),
        'state_hgrn': nrm((DEC_BATCH, N_HGRN_LAYERS, 2, HGRN_HEADS, HGRN_DK, HGRN_DV), 0.5),
        'c_ctx': nrm((D,), 1.0),
        'w_mod': nrm((DEPTH, D, 9 * D), 0.5 * D ** -0.5),
        'b_mod': nrm((DEPTH, 9 * D), 0.02),
        'norm_g': gain((DEPTH, 3, 2, D)),
        'w_ffn_in': nrm((DEPTH, 2, D, 2 * D_FF), D ** -0.5),
        'w_ffn_out': nrm((DEPTH, 2, D_FF, D), D_FF ** -0.5),
        'w_attn_qkv': nrm((N_ATTN_LAYERS, D, 3 * D), D ** -0.5),
        'w_attn_o': nrm((N_ATTN_LAYERS, D, D), D ** -0.5),
        'attn_lambda': nrm((N_ATTN_LAYERS, 4, ATTN_HEAD_DIM), 0.1),
        'attn_subln_g': gain((N_ATTN_LAYERS, HV)),
        'w_hgrn_in': nrm((N_HGRN_LAYERS, D, 5 * D), D ** -0.5),
        'w_hgrn_o': nrm((N_HGRN_LAYERS, D, D), D ** -0.5),
        'hgrn_norm_g': gain((N_HGRN_LAYERS, HGRN_DV)),
        'hgrn_lb': nrm((2, DEPTH, D), 0.5),
        'w_cm_in': nrm((N_CONV_LAYERS, D, 2 * D), D ** -0.5),
        'b_cm_in': nrm((N_CONV_LAYERS, 2 * D), 0.02),
        'w_cm_dw': nrm((N_CONV_LAYERS, CONV_WIDTH, D), CONV_WIDTH ** -0.5),
        'b_cm_dw': nrm((N_CONV_LAYERS, D), 0.02),
        'cm_ln_g': gain((N_CONV_LAYERS, D)),
        'cm_ln_b': nrm((N_CONV_LAYERS, D), 0.02),
        'w_cm_out': nrm((N_CONV_LAYERS, D, D), D ** -0.5),
        'b_cm_out': nrm((N_CONV_LAYERS, D), 0.02),
        'w_sc_in': nrm((N_SCONV_LAYERS, D, 3 * D), D ** -0.5),
        'w_sc_conv': nrm((N_SCONV_LAYERS, SHORT_CONV_WIDTH, D), SHORT_CONV_WIDTH ** -0.5),
        'w_sc_out': nrm((N_SCONV_LAYERS, D, D), D ** -0.5),
    }


def reference(x_prompt, x_sample, c, cache_k, cache_v, state_hgrn, c_ctx, w_mod, b_mod, norm_g,
              w_ffn_in, w_ffn_out, w_attn_qkv, w_attn_o, attn_lambda, attn_subln_g,
              w_hgrn_in, w_hgrn_o, hgrn_norm_g, hgrn_lb, w_cm_in, b_cm_in, w_cm_dw, b_cm_dw,
              cm_ln_g, cm_ln_b, w_cm_out, b_cm_out, w_sc_in, w_sc_conv, w_sc_out):
    rope = axial_rope_tables(x_sample.shape[1])
    p_lb = jax.nn.softmax(hgrn_lb.astype(jnp.float32), axis=1)
    lower_bounds = jnp.cumsum(p_lb, axis=1) - p_lb[:, :1]
    ctx_zero = jnp.zeros((x_prompt.shape[0], HGRN_HEADS, HGRN_DK, HGRN_DV), jnp.float32)

    y_p, y_s = x_prompt, x_sample
    new_k, new_v, new_s = [], [], []
    for i in range(DEPTH):
        kind, j = i % N_MIXERS, i // N_MIXERS
        g = norm_g[i]
        m_p = modulation(c_ctx[None], w_mod[i], b_mod[i])
        m_s = modulation(c, w_mod[i], b_mod[i])

        y_p = post_add(y_p, swiglu(pre_norm(y_p, m_p, 0, g[0, 0]), w_ffn_in[i, 0], w_ffn_out[i, 0]), m_p, 0, g[0, 1], 0.5)
        y_s = post_add(y_s, swiglu(pre_norm(y_s, m_s, 0, g[0, 0]), w_ffn_in[i, 0], w_ffn_out[i, 0]), m_s, 0, g[0, 1], 0.5)

        h_p = pre_norm(y_p, m_p, 1, g[1, 0])
        h_s = pre_norm(y_s, m_s, 1, g[1, 0])
        if kind == 0:
            o_p, k_p, v_p = diff_attention(h_p, None, None, None, w_attn_qkv[j], w_attn_o[j],
                                           attn_lambda[j], attn_subln_g[j], i)
            o_s, _, _ = diff_attention(h_s, cache_k[:, j], cache_v[:, j], rope, w_attn_qkv[j], w_attn_o[j],
                                       attn_lambda[j], attn_subln_g[j], i)
            new_k.append(k_p)
            new_v.append(v_p)
        elif kind == 1:
            o_p, st_p = hgrn2_mixer(h_p, ctx_zero, ctx_zero, w_hgrn_in[j], w_hgrn_o[j], hgrn_norm_g[j],
                                    lower_bounds[:, i])
            o_s, _ = hgrn2_mixer(h_s, state_hgrn[:, j, 0], state_hgrn[:, j, 1], w_hgrn_in[j], w_hgrn_o[j],
                                 hgrn_norm_g[j], lower_bounds[:, i])
            new_s.append(st_p)
        elif kind == 2:
            o_p = conformer_conv(h_p, w_cm_in[j], b_cm_in[j], w_cm_dw[j], b_cm_dw[j], cm_ln_g[j], cm_ln_b[j],
                                 w_cm_out[j], b_cm_out[j])
            o_s = conformer_conv(h_s, w_cm_in[j], b_cm_in[j], w_cm_dw[j], b_cm_dw[j], cm_ln_g[j], cm_ln_b[j],
                                 w_cm_out[j], b_cm_out[j])
        else:
            o_p = short_gated_conv(h_p, w_sc_in[j], w_sc_conv[j], w_sc_out[j])
            o_s = short_gated_conv(h_s, w_sc_in[j], w_sc_conv[j], w_sc_out[j])
        y_p = post_add(y_p, o_p, m_p, 1, g[1, 1], 1.0)
        y_s = post_add(y_s, o_s, m_s, 1, g[1, 1], 1.0)

        y_p = post_add(y_p, swiglu(pre_norm(y_p, m_p, 2, g[2, 0]), w_ffn_in[i, 1], w_ffn_out[i, 1]), m_p, 2, g[2, 1], 0.5)
        y_s = post_add(y_s, swiglu(pre_norm(y_s, m_s, 2, g[2, 0]), w_ffn_in[i, 1], w_ffn_out[i, 1]), m_s, 2, g[2, 1], 0.5)

    new_cache_k = jnp.stack(new_k, axis=1)
    new_cache_v = jnp.stack(new_v, axis=1)
    new_state_hgrn = jnp.stack(new_s, axis=1)
    return (y_p, y_s, new_cache_k, new_cache_v, new_state_hgrn)
```

```python
import functools
import math

import jax
import jax.numpy as jnp
from jax import lax
from jax.experimental import pallas as pl
from jax.experimental.pallas import tpu as pltpu

D_MODEL = 1024
DEPTH = 4
GRID_W = 64
N_MIXERS = 4
ATTN_HEAD_DIM = 64
ATTN_HEADS = D_MODEL // (2 * ATTN_HEAD_DIM)
HEAD_W = 2 * ATTN_HEAD_DIM
ROPE_BASE = 10000.0
HGRN_DK = 128
HGRN_HEADS = D_MODEL // HGRN_DK
HGRN_DV = D_MODEL // HGRN_HEADS
CONV_WIDTH = 31
SHORT_CONV_WIDTH = 3
D_FF = ((8 * D_MODEL // 3 + 127) // 128) * 128
EPS = 1e-6

F32 = jnp.float32
BF16 = jnp.bfloat16

TOKEN_TILE = 512
FF_TILE = D_FF // 2
ATTN_Q_TILE = 256
GLA_CHUNK = 128
GLA_DIAG = 8
CONV_TILE = 256
CONV_HALO = 16
MOD_TILE = 2304
VMEM_LIMIT = 56 * 1024 * 1024


def _bdot(a, b):
    return jnp.dot(a, b, preferred_element_type=F32)


def _bdot_nt(a, b):
    return lax.dot_general(a, b, (((1,), (1,)), ((), ())), preferred_element_type=F32)


def _rms(x, g):
    return x * lax.rsqrt(jnp.mean(x * x, axis=-1, keepdims=True) + EPS) * g


def _silu(x):
    return x * jax.nn.sigmoid(x)


def _split2(x):
    hi = x.astype(BF16)
    lo = (x - hi.astype(F32)).astype(BF16)
    return hi, lo


def _split3(x):
    hi = x.astype(BF16)
    r = x - hi.astype(F32)
    mid = r.astype(BF16)
    lo = (r - mid.astype(F32)).astype(BF16)
    return hi, mid, lo


def _params(sem, vmem=None):
    return pltpu.CompilerParams(dimension_semantics=sem, vmem_limit_bytes=vmem)


def _mod_kernel(c_ref, w_ref, b_ref, o_ref):
    a_hi, a_lo = _split2(_silu(c_ref[...]))
    w_hi, w_lo = _split2(w_ref[0])
    acc = _bdot(a_hi, w_hi) + _bdot(a_lo, w_hi) + _bdot(a_hi, w_lo)
    o_ref[0] = acc + b_ref[0]


def _modulation(cvec, w_mod, b_mod):
    rows = cvec.shape[0]
    n = w_mod.shape[-1]
    return pl.pallas_call(
        _mod_kernel,
        grid=(DEPTH, n // MOD_TILE),
        in_specs=[
            pl.BlockSpec((rows, D_MODEL), lambda i, j: (0, 0)),
            pl.BlockSpec((1, D_MODEL, MOD_TILE), lambda i, j: (i, 0, j)),
            pl.BlockSpec((1, 1, MOD_TILE), lambda i, j: (i, 0, j)),
        ],
        out_specs=pl.BlockSpec((1, rows, MOD_TILE), lambda i, j: (i, 0, j)),
        out_shape=jax.ShapeDtypeStruct((DEPTH, rows, n), F32),
        compiler_params=_params(("arbitrary", "arbitrary"), VMEM_LIMIT),
        name="modulation",
    )(cvec, w_mod, b_mod.reshape(DEPTH, 1, n))


class _Stream:
    def __init__(self, batch, seq, mod_row0, per_batch_mod):
        self.batch = batch
        self.seq = seq
        self.rows = batch * seq
        self.mod_row0 = mod_row0
        self.tokens_per_mod = seq if per_batch_mod else batch * seq

    def mod_spec(self, tile):
        row0, tpm = self.mod_row0, self.tokens_per_mod
        assert tpm % tile == 0
        return pl.BlockSpec((1, 3, D_MODEL), lambda t, *_: (row0 + (t * tile) // tpm, 0, 0))


def _ffn_kernel(x_ref, m_ref, g_ref, wg_ref, wu_ref, wo_ref, o_ref, h_scr, acc_scr, *, weight):
    j = pl.program_id(1)

    @pl.when(j == 0)
    def _():
        m = m_ref[0]
        h = _rms(x_ref[...], g_ref[0:1, :]) * (1.0 + m[1:2, :]) + m[0:1, :]
        h_scr[...] = h.astype(BF16)
        acc_scr[...] = jnp.zeros_like(acc_scr)

    h = h_scr[...]
    gate = _bdot(h, wg_ref[...])
    up = _bdot(h, wu_ref[...])
    act = (_silu(gate) * up).astype(BF16)
    acc_scr[...] += _bdot(act, wo_ref[...])

    @pl.when(j == pl.num_programs(1) - 1)
    def _():
        m = m_ref[0]
        o_ref[...] = x_ref[...] + weight * m[2:3, :] * _rms(acc_scr[...], g_ref[1:2, :])


def _ffn(stream, y, m, g, w_in, w_out, weight):
    tm, tf = TOKEN_TILE, FF_TILE
    nf = D_FF // tf
    return pl.pallas_call(
        functools.partial(_ffn_kernel, weight=weight),
        grid=(stream.rows // tm, nf),
        in_specs=[
            pl.BlockSpec((tm, D_MODEL), lambda t, j: (t, 0)),
            stream.mod_spec(tm),
            pl.BlockSpec((2, D_MODEL), lambda t, j: (0, 0)),
            pl.BlockSpec((D_MODEL, tf), lambda t, j: (0, j)),
            pl.BlockSpec((D_MODEL, tf), lambda t, j: (0, j + nf)),
            pl.BlockSpec((tf, D_MODEL), lambda t, j: (j, 0)),
        ],
        out_specs=pl.BlockSpec((tm, D_MODEL), lambda t, j: (t, 0)),
        out_shape=jax.ShapeDtypeStruct((stream.rows, D_MODEL), F32),
        scratch_shapes=[pltpu.VMEM((tm, D_MODEL), BF16), pltpu.VMEM((tm, D_MODEL), F32)],
        compiler_params=_params(("parallel", "arbitrary"), VMEM_LIMIT),
        name="ffn",
    )(y, m, g, w_in, w_in, w_out)


def _proj_kernel(*refs, mode, has_bias):
    x_ref, m_ref, g_ref, w_ref = refs[:4]
    b_ref = refs[4] if has_bias else None
    outs = refs[5:] if has_bias else refs[4:]
    m = m_ref[0]
    h = (_rms(x_ref[...], g_ref[0:1, :]) * (1.0 + m[1:2, :]) + m[0:1, :]).astype(BF16)

    def part(p):
        r = _bdot(h, w_ref[:, p * D_MODEL:(p + 1) * D_MODEL])
        if has_bias:
            r = r + b_ref[:, p * D_MODEL:(p + 1) * D_MODEL]
        return r

    if mode == "split":
        for p, o_ref in enumerate(outs):
            o_ref[...] = part(p)
    elif mode == "glu":
        a = part(0)
        outs[0][...] = a * jax.nn.sigmoid(part(1))
    else:
        outs[0][...] = part(0)
        outs[1][...] = part(1) * part(2)


def _proj(stream, y, m, g, w, b, mode, n_out):
    tm = TOKEN_TILE
    n = w.shape[1]
    has_bias = b is not None
    in_specs = [
        pl.BlockSpec((tm, D_MODEL), lambda t: (t, 0)),
        stream.mod_spec(tm),
        pl.BlockSpec((2, D_MODEL), lambda t: (0, 0)),
        pl.BlockSpec((D_MODEL, n), lambda t: (0, 0)),
    ]
    args = [y, m, g, w]
    if has_bias:
        in_specs.append(pl.BlockSpec((1, n), lambda t: (0, 0)))
        args.append(b.reshape(1, n))
    return pl.pallas_call(
        functools.partial(_proj_kernel, mode=mode, has_bias=has_bias),
        grid=(stream.rows // tm,),
        in_specs=in_specs,
        out_specs=[pl.BlockSpec((tm, D_MODEL), lambda t: (t, 0))] * n_out,
        out_shape=[jax.ShapeDtypeStruct((stream.rows, D_MODEL), F32)] * n_out,
        compiler_params=_params(("parallel",), VMEM_LIMIT),
        name="proj_" + mode,
    )(*args)


def _outproj_kernel(*refs, has_bias):
    mix_ref, w_ref = refs[:2]
    b_ref = refs[2] if has_bias else None
    x_ref, m_ref, g_ref, o_ref = refs[3:] if has_bias else refs[2:]
    out = _bdot(mix_ref[...], w_ref[...])
    if has_bias:
        out = out + b_ref[...]
    o_ref[...] = x_ref[...] + m_ref[0][2:3, :] * _rms(out, g_ref[1:2, :])


def _outproj(stream, mix, w, b, y, m, g):
    tm = TOKEN_TILE
    has_bias = b is not None
    in_specs = [
        pl.BlockSpec((tm, D_MODEL), lambda t: (t, 0)),
        pl.BlockSpec((D_MODEL, D_MODEL), lambda t: (0, 0)),
    ]
    args = [mix, w]
    if has_bias:
        in_specs.append(pl.BlockSpec((1, D_MODEL), lambda t: (0, 0)))
        args.append(b.reshape(1, D_MODEL))
    in_specs += [
        pl.BlockSpec((tm, D_MODEL), lambda t: (t, 0)),
        stream.mod_spec(tm),
        pl.BlockSpec((2, D_MODEL), lambda t: (0, 0)),
    ]
    args += [y, m, g]
    return pl.pallas_call(
        functools.partial(_outproj_kernel, has_bias=has_bias),
        grid=(stream.rows // tm,),
        in_specs=in_specs,
        out_specs=pl.BlockSpec((tm, D_MODEL), lambda t: (t, 0)),
        out_shape=jax.ShapeDtypeStruct((stream.rows, D_MODEL), F32),
        compiler_params=_params(("parallel",), VMEM_LIMIT),
        name="outproj",
    )(*args)


def _rope(x, cos, sin_signed):
    lane = lax.broadcasted_iota(jnp.int32, x.shape, 1)
    first = (lane % (ATTN_HEAD_DIM // 2)) < (ATTN_HEAD_DIM // 4)
    rot = jnp.where(first, pltpu.roll(x, HEAD_W - ATTN_HEAD_DIM // 4, 1),
                    pltpu.roll(x, ATTN_HEAD_DIM // 4, 1))
    return x * cos + rot * sin_signed


def _attn_kernel(*refs, has_ctx, lam_init, seq, past):
    if has_ctx:
        (q_ref, k_ref, v_ref, ck_ref, cv_ref, cosq_ref, sinq_ref, cosk_ref, sink_ref,
         lam_ref, g_ref, o_ref, k_scr, v_scr) = refs
    else:
        q_ref, k_ref, v_ref, lam_ref, g_ref, o_ref, k_scr, v_scr = refs

    @pl.when(pl.program_id(2) == 0)
    def _():
        if has_ctx:
            k_scr[0:past, :] = ck_ref[0].astype(BF16)
            v_scr[0:past, :] = cv_ref[0].astype(BF16)
            k_scr[past:past + seq, :] = _rope(k_ref[...], cosk_ref[...], sink_ref[...]).astype(BF16)
            v_scr[past:past + seq, :] = v_ref[...].astype(BF16)
        else:
            k_scr[...] = k_ref[...].astype(BF16)
            v_scr[...] = v_ref[...].astype(BF16)

    q = q_ref[...]
    if has_ctx:
        q = _rope(q, cosq_ref[...], sinq_ref[...])
    q = q * (ATTN_HEAD_DIM ** -0.5)
    lane = lax.broadcasted_iota(jnp.int32, q.shape, 1)
    k = k_scr[...]
    v = v_scr[...]

    def softmax_pv(qm):
        s = _bdot_nt(qm.astype(BF16), k)
        p = jnp.exp(s - jnp.max(s, axis=-1, keepdims=True))
        l = jnp.sum(p, axis=-1, keepdims=True)
        return _bdot(p.astype(BF16), v) / l

    lp = lam_ref[...]
    lam = (jnp.exp(jnp.sum(lp[0:1, :] * lp[1:2, :], axis=-1, keepdims=True))
           - jnp.exp(jnp.sum(lp[2:3, :] * lp[3:4, :], axis=-1, keepdims=True)) + lam_init)
    o = (softmax_pv(jnp.where(lane < ATTN_HEAD_DIM, q, 0.0))
         - lam * softmax_pv(jnp.where(lane >= ATTN_HEAD_DIM, q, 0.0)))
    o_ref[...] = (_rms(o, g_ref[...]) * (1.0 - lam_init)).astype(BF16)


def _attention(stream, q, k, v, lam_params, subln_g, layer_idx, ctx=None):
    seq, tq = stream.seq, ATTN_Q_TILE
    nq = seq // tq
    lam_init = 0.8 - 0.6 * math.exp(-0.3 * layer_idx)
    has_ctx = ctx is not None
    in_specs = [
        pl.BlockSpec((tq, HEAD_W), lambda b, h, i: (b * nq + i, h)),
        pl.BlockSpec((seq, HEAD_W), lambda b, h, i: (b, h)),
        pl.BlockSpec((seq, HEAD_W), lambda b, h, i: (b, h)),
    ]
    args = [q, k, v]
    past = 0
    if has_ctx:
        ck, cv, cos, sin_signed = ctx
        past = ck.shape[1]
        in_specs += [
            pl.BlockSpec((1, past, HEAD_W), lambda b, h, i: (b, 0, h)),
            pl.BlockSpec((1, past, HEAD_W), lambda b, h, i: (b, 0, h)),
            pl.BlockSpec((tq, HEAD_W), lambda b, h, i: (i, 0)),
            pl.BlockSpec((tq, HEAD_W), lambda b, h, i: (i, 0)),
            pl.BlockSpec((seq, HEAD_W), lambda b, h, i: (0, 0)),
            pl.BlockSpec((seq, HEAD_W), lambda b, h, i: (0, 0)),
        ]
        args += [ck, cv, cos, sin_signed, cos, sin_signed]
    in_specs += [
        pl.BlockSpec((4, ATTN_HEAD_DIM), lambda b, h, i: (0, 0)),
        pl.BlockSpec((1, HEAD_W), lambda b, h, i: (0, 0)),
    ]
    args += [lam_params, subln_g.reshape(1, HEAD_W)]
    return pl.pallas_call(
        functools.partial(_attn_kernel, has_ctx=has_ctx, lam_init=lam_init, seq=seq, past=past),
        grid=(stream.batch, ATTN_HEADS, nq),
        in_specs=in_specs,
        out_specs=pl.BlockSpec((tq, HEAD_W), lambda b, h, i: (b * nq + i, h)),
        out_shape=jax.ShapeDtypeStruct((stream.rows, D_MODEL), BF16),
        scratch_shapes=[pltpu.VMEM((past + seq, HEAD_W), BF16), pltpu.VMEM((past + seq, HEAD_W), BF16)],
        compiler_params=_params(("parallel", "parallel", "arbitrary"), VMEM_LIMIT),
        name="diff_attention",
    )(*args)


def _rope_tables(n_tokens):
    rows = n_tokens // GRID_W
    row = jnp.repeat(jnp.arange(rows), GRID_W).astype(F32)
    col = jnp.tile(jnp.arange(GRID_W), rows).astype(F32)
    n_freq = ATTN_HEAD_DIM // 4
    inv_freq = ROPE_BASE ** (-jnp.arange(n_freq, dtype=F32) / n_freq)
    ang_r = row[:, None] * inv_freq
    ang_c = col[:, None] * inv_freq
    ang = jnp.concatenate([ang_r, ang_r, ang_c, ang_c], axis=-1)
    sign = jnp.concatenate([-jnp.ones((n_freq,), F32), jnp.ones((n_freq,), F32)] * 2)
    cos = jnp.cos(ang)
    sin_signed = jnp.sin(ang) * sign
    return jnp.tile(cos, (1, 2)), jnp.tile(sin_signed, (1, 2))


def _gla_gates(z, lbd):
    log_sig = jnp.minimum(z, 0.0) - jnp.log1p(jnp.exp(-jnp.abs(z)))
    a = jnp.log(lbd)
    c = jnp.log1p(-lbd) + log_sig
    logf = jnp.maximum(a, c) + jnp.log1p(jnp.exp(-jnp.abs(a - c)))
    k = (1.0 - lbd) * jax.nn.sigmoid(-z)
    return logf, k


def _gla_chunk(q, k, v, logf, st, tri, fwd):
    c = q.shape[0]
    hi, mid, lo = _split3(logf)
    b = _bdot(tri, hi) + _bdot(tri, mid) + _bdot(tri, lo)
    b_last = b[c - 1:c, :] if fwd else b[0:1, :]
    vb = v.astype(BF16)

    o = _bdot_nt((q * jnp.exp(b)).astype(BF16), st.astype(BF16))
    ke = (k * jnp.exp(b_last - b)).astype(BF16)
    st_new = st * jnp.exp(b_last) + _bdot(v.T.astype(BF16), ke)

    row = lax.broadcasted_iota(jnp.int32, (c, HGRN_DK), 0)
    pair = (lax.broadcasted_iota(jnp.int32, (c, c), 0) ^ lax.broadcasted_iota(jnp.int32, (c, c), 1))
    sc = jnp.zeros((c, c), F32)
    m = GLA_DIAG
    while m < c:
        n = c // (2 * m)
        b3 = b.reshape(n, 2 * m, HGRN_DK)
        br = b3[:, m - 1:m, :] if fwd else b3[:, m:m + 1, :]
        ex = jnp.exp(-jnp.abs(b3 - br)).reshape(c, HGRN_DK)
        second = (row & m) != 0
        q_side = second if fwd else jnp.logical_not(second)
        qs = jnp.where(q_side, q * ex, 0.0).astype(BF16)
        ks = jnp.where(q_side, 0.0, k * ex).astype(BF16)
        sc = jnp.where((pair >> int(math.log2(m))) == 1, _bdot_nt(qs, ks), sc)
        m *= 2
    o = o + _bdot(sc.astype(BF16), vb)

    nb = c // GLA_DIAG
    q3 = q.reshape(nb, GLA_DIAG, HGRN_DK)
    k3 = k.reshape(nb, GLA_DIAG, HGRN_DK)
    b3 = b.reshape(nb, GLA_DIAG, HGRN_DK)
    v3 = v.reshape(nb, GLA_DIAG, HGRN_DV)
    trow = lax.broadcasted_iota(jnp.int32, (nb, GLA_DIAG, HGRN_DK), 1)
    od = jnp.zeros((nb, GLA_DIAG, HGRN_DV), F32)
    for s in range(GLA_DIAG):
        w = q3 * k3[:, s:s + 1, :] * jnp.exp(b3 - b3[:, s:s + 1, :])
        valid = (trow >= s) if fwd else (trow <= s)
        col = jnp.sum(jnp.where(valid, w, 0.0), axis=-1, keepdims=True)
        od = od + col * v3[:, s:s + 1, :]
    return o + od.reshape(c, HGRN_DV), st_new


def _hgrn_kernel(*refs, layer_idx, has_state, emit_state, seq):
    q_ref, zf_ref, zb_ref, i_ref, g_ref, lb_ref, ng_ref = refs[:7]
    rest = list(refs[7:])
    s0f_ref = rest.pop(0) if has_state else None
    s0b_ref = rest.pop(0) if has_state else None
    o_ref = rest.pop(0)
    sfin_ref = rest.pop(0) if emit_state else None
    o_scr, st_scr = rest
    c = GLA_CHUNK
    n_chunks = seq // c

    ti = lax.broadcasted_iota(jnp.int32, (c, c), 0)
    si = lax.broadcasted_iota(jnp.int32, (c, c), 1)

    for d, (z_ref, s0_ref) in enumerate(((zf_ref, s0f_ref), (zb_ref, s0b_ref))):
        fwd = d == 0
        lbp = lb_ref[d]
        e = jnp.exp(lbp - jnp.max(lbp, axis=0, keepdims=True))
        p = e / jnp.sum(e, axis=0, keepdims=True)
        lbd = jnp.sum(p[1:layer_idx + 1, :], axis=0, keepdims=True)
        tri = jnp.where((si <= ti) if fwd else (si >= ti), 1.0, 0.0).astype(BF16)
        st_scr[...] = s0_ref[...].T if has_state else jnp.zeros_like(st_scr)

        def body(i, carry, z_ref=z_ref, fwd=fwd, lbd=lbd, tri=tri):
            ci = i if fwd else n_chunks - 1 - i
            rows = pl.ds(pl.multiple_of(ci * c, c), c)
            logf, k = _gla_gates(z_ref[rows, :], lbd)
            o, st_new = _gla_chunk(q_ref[rows, :], k, i_ref[rows, :], logf, st_scr[...], tri, fwd)
            st_scr[...] = st_new
            if fwd:
                o_scr[rows, :] = o
            else:
                o_scr[rows, :] += o
            return carry

        lax.fori_loop(0, n_chunks, body, 0)
        if emit_state:
            sfin_ref[d] = st_scr[...].T

    o_ref[...] = (_rms(o_scr[...], ng_ref[...]) * _silu(g_ref[...])).astype(BF16)


def _hgrn(stream, parts, lb, norm_g, layer_idx, state=None, state_layer=0, emit_state=False):
    seq = stream.seq
    has_state = state is not None
    col = pl.BlockSpec((seq, HGRN_DK), lambda b, h: (b, h))
    in_specs = [col] * 5 + [
        pl.BlockSpec((2, DEPTH, HGRN_DK), lambda b, h: (0, 0, h)),
        pl.BlockSpec((1, HGRN_DV), lambda b, h: (0, 0)),
    ]
    args = list(parts) + [lb, norm_g.reshape(1, HGRN_DV)]
    if has_state:
        for d in range(2):
            in_specs.append(pl.BlockSpec((None, None, None, None, HGRN_DK, HGRN_DV),
                                         lambda b, h, d=d: (b, state_layer, d, h, 0, 0)))
            args.append(state)
    out_specs = [pl.BlockSpec((seq, HGRN_DV), lambda b, h: (b, h))]
    out_shape = [jax.ShapeDtypeStruct((stream.rows, D_MODEL), BF16)]
    if emit_state:
        out_specs.append(pl.BlockSpec((None, 2, None, HGRN_DK, HGRN_DV), lambda b, h: (b, 0, h, 0, 0)))
        out_shape.append(jax.ShapeDtypeStruct((stream.batch, 2, HGRN_HEADS, HGRN_DK, HGRN_DV), F32))
    res = pl.pallas_call(
        functools.partial(_hgrn_kernel, layer_idx=layer_idx, has_state=has_state,
                          emit_state=emit_state, seq=seq),
        grid=(stream.batch, HGRN_HEADS),
        in_specs=in_specs,
        out_specs=out_specs,
        out_shape=out_shape,
        scratch_shapes=[pltpu.VMEM((seq, HGRN_DV), F32), pltpu.VMEM((HGRN_DV, HGRN_DK), F32)],
        compiler_params=_params(("parallel", "parallel"), VMEM_LIMIT),
        name="hgrn2",
    )(*args)
    return res if emit_state else (res[0], None)


def _fill_padded(pad_scr, src_ref, seq):
    zeros = jnp.zeros((CONV_HALO, D_MODEL), F32)
    pad_scr[0:CONV_HALO, :] = zeros
    pad_scr[CONV_HALO:CONV_HALO + seq, :] = src_ref[...]
    pad_scr[CONV_HALO + seq:2 * CONV_HALO + seq, :] = zeros


def _dwconv_tile(pad_scr, w_ref, width):
    tile = CONV_TILE
    base = pl.multiple_of(pl.program_id(1) * tile, tile)
    win = pad_scr[pl.ds(base, tile + 2 * CONV_HALO), :]
    acc = None
    for j in range(width):
        o = CONV_HALO - width // 2 + j
        term = win[o:o + tile, :] * w_ref[j:j + 1, :]
        acc = term if acc is None else acc + term
    return acc


def _conformer_conv_kernel(u_ref, w_ref, bdw_ref, lng_ref, lnb_ref, o_ref, pad_scr, *, seq):
    @pl.when(pl.program_id(1) == 0)
    def _():
        _fill_padded(pad_scr, u_ref, seq)

    u = _dwconv_tile(pad_scr, w_ref, CONV_WIDTH) + bdw_ref[...]
    xc = u - jnp.mean(u, axis=-1, keepdims=True)
    y = xc * lax.rsqrt(jnp.mean(xc * xc, axis=-1, keepdims=True) + EPS) * lng_ref[...] + lnb_ref[...]
    o_ref[...] = _silu(y).astype(BF16)


def _short_conv_kernel(p_ref, bg_ref, w_ref, o_ref, pad_scr, *, seq):
    @pl.when(pl.program_id(1) == 0)
    def _():
        _fill_padded(pad_scr, p_ref, seq)

    o_ref[...] = (bg_ref[...] * _dwconv_tile(pad_scr, w_ref, SHORT_CONV_WIDTH)).astype(BF16)


def _conv_call(stream, body, seq_inputs, tile_inputs, small_inputs):
    seq, tile = stream.seq, CONV_TILE
    nt = seq // tile
    in_specs = ([pl.BlockSpec((seq, D_MODEL), lambda b, t: (b, 0))] * len(seq_inputs)
                + [pl.BlockSpec((tile, D_MODEL), lambda b, t: (b * nt + t, 0))] * len(tile_inputs)
                + [pl.BlockSpec(a.shape, lambda b, t: (0, 0)) for a in small_inputs])
    return pl.pallas_call(
        functools.partial(body, seq=seq),
        grid=(stream.batch, nt),
        in_specs=in_specs,
        out_specs=pl.BlockSpec((tile, D_MODEL), lambda b, t: (b * nt + t, 0)),
        out_shape=jax.ShapeDtypeStruct((stream.rows, D_MODEL), BF16),
        scratch_shapes=[pltpu.VMEM((seq + 2 * CONV_HALO, D_MODEL), F32)],
        compiler_params=_params(("parallel", "arbitrary"), VMEM_LIMIT),
        name="dwconv",
    )(*seq_inputs, *tile_inputs, *small_inputs)


def kernel(x_prompt, x_sample, c, cache_k, cache_v, state_hgrn, c_ctx, w_mod, b_mod, norm_g,
           w_ffn_in, w_ffn_out, w_attn_qkv, w_attn_o, attn_lambda, attn_subln_g,
           w_hgrn_in, w_hgrn_o, hgrn_norm_g, hgrn_lb, w_cm_in, b_cm_in, w_cm_dw, b_cm_dw,
           cm_ln_g, cm_ln_b, w_cm_out, b_cm_out, w_sc_in, w_sc_conv, w_sc_out):
    batch, seq, _ = x_prompt.shape
    dec_batch, dec_seq, _ = x_sample.shape
    past = cache_k.shape[2]
    prompt = _Stream(batch, seq, 0, per_batch_mod=False)
    sample = _Stream(dec_batch, dec_seq, 1, per_batch_mod=True)
    streams = (prompt, sample)

    cvec = jnp.concatenate([c_ctx[None], c, jnp.zeros((8 - 1 - dec_batch, D_MODEL), F32)], axis=0)
    mods = _modulation(cvec, w_mod, b_mod).reshape(DEPTH, 8, 3, 3, D_MODEL)

    w_ffn_in_b = w_ffn_in.astype(BF16)
    w_ffn_out_b = w_ffn_out.astype(BF16)
    rope = _rope_tables(dec_seq)

    ys = [x_prompt.reshape(prompt.rows, D_MODEL), x_sample.reshape(sample.rows, D_MODEL)]
    new_k = new_v = new_s = None
    for i in range(DEPTH):
        kind, j = i % N_MIXERS, i // N_MIXERS
        g = norm_g[i]

        def sub_mod(s):
            return mods[i, :, s]

        for si, st in enumerate(streams):
            ys[si] = _ffn(st, ys[si], sub_mod(0), g[0], w_ffn_in_b[i, 0], w_ffn_out_b[i, 0], 0.5)

        m1 = sub_mod(1)
        for si, st in enumerate(streams):
            y = ys[si]
            is_prompt = si == 0
            bias_out = None
            if kind == 0:
                q, k, v = _proj(st, y, m1, g[1], w_attn_qkv[j].astype(BF16), None, "split", 3)
                if is_prompt:
                    ctx = None
                    new_k = k.reshape(batch, 1, seq, ATTN_HEADS, HEAD_W)
                    new_v = v.reshape(batch, 1, seq, ATTN_HEADS, HEAD_W)
                else:
                    ctx = (cache_k[:, j].reshape(dec_batch, past, D_MODEL),
                           cache_v[:, j].reshape(dec_batch, past, D_MODEL), rope[0], rope[1])
                mix = _attention(st, q, k, v, attn_lambda[j], attn_subln_g[j], i, ctx)
                w_o = w_attn_o[j]
            elif kind == 1:
                parts = _proj(st, y, m1, g[1], w_hgrn_in[j].astype(BF16), None, "split", 5)
                if is_prompt:
                    mix, s_fin = _hgrn(st, parts, hgrn_lb, hgrn_norm_g[j], i, emit_state=True)
                    new_s = s_fin[:, None]
                else:
                    mix, _ = _hgrn(st, parts, hgrn_lb, hgrn_norm_g[j], i, state=state_hgrn, state_layer=j)
                w_o = w_hgrn_o[j]
            elif kind == 2:
                (u,) = _proj(st, y, m1, g[1], w_cm_in[j].astype(BF16), b_cm_in[j], "glu", 1)
                mix = _conv_call(st, _conformer_conv_kernel, [u], [],
                                 [w_cm_dw[j], b_cm_dw[j].reshape(1, D_MODEL),
                                  cm_ln_g[j].reshape(1, D_MODEL), cm_ln_b[j].reshape(1, D_MODEL)])
                w_o = w_cm_out[j]
                bias_out = b_cm_out[j]
            else:
                bg, prod = _proj(st, y, m1, g[1], w_sc_in[j].astype(BF16), None, "gated_pair", 2)
                mix = _conv_call(st, _short_conv_kernel, [prod], [bg], [w_sc_conv[j]])
                w_o = w_sc_out[j]
            ys[si] = _outproj(st, mix, w_o.astype(BF16), bias_out, y, m1, g[1])

        for si, st in enumerate(streams):
            ys[si] = _ffn(st, ys[si], sub_mod(2), g[2], w_ffn_in_b[i, 1], w_ffn_out_b[i, 1], 0.5)

    return (ys[0].reshape(batch, seq, D_MODEL), ys[1].reshape(dec_batch, dec_seq, D_MODEL),
            new_k, new_v, new_s)
```

```python
import functools
import math

import jax
import jax.numpy as jnp
from jax import lax
from jax.experimental import pallas as pl
from jax.experimental.pallas import tpu as pltpu

D_MODEL = 1024
DEPTH = 4
GRID_W = 64
N_MIXERS = 4
ATTN_HEAD_DIM = 64
ATTN_HEADS = D_MODEL // (2 * ATTN_HEAD_DIM)
HEAD_W = 2 * ATTN_HEAD_DIM
ROPE_BASE = 10000.0
HGRN_DK = 128
HGRN_HEADS = D_MODEL // HGRN_DK
HGRN_DV = D_MODEL // HGRN_HEADS
CONV_WIDTH = 31
SHORT_CONV_WIDTH = 3
D_FF = ((8 * D_MODEL // 3 + 127) // 128) * 128
EPS = 1e-6

F32 = jnp.float32
BF16 = jnp.bfloat16

SUBLANES = 8
FFN_TILE = 1024
TOKEN_TILE = 512
SUB_ROWS = 256
ATTN_Q_TILE = 256
ATTN_KEY_BLOCK = 512
ATTN_HEAD_GROUP = 2
GLA_CHUNK = 128
CONV_TILE = 256
CONV_HALO = 16
MOD_TILE = 2304
VMEM_LIMIT = 56 * 1024 * 1024
LOGF_MIN = -180.0
GLA_HEADS = 2


def _bdot(a, b):
    return jnp.dot(a, b, preferred_element_type=F32)


def _bdot_nt(a, b):
    return lax.dot_general(a, b, (((1,), (1,)), ((), ())), preferred_element_type=F32)


def _rms(x, g):
    return x * lax.rsqrt(jnp.mean(x * x, axis=-1, keepdims=True) + EPS) * g


def _silu(x):
    return x * jax.nn.sigmoid(x)


def _split2(x):
    hi = x.astype(BF16)
    lo = (x - hi.astype(F32)).astype(BF16)
    return hi, lo


def _split3(x):
    hi = x.astype(BF16)
    r = x - hi.astype(F32)
    mid = r.astype(BF16)
    lo = (r - mid.astype(F32)).astype(BF16)
    return hi, mid, lo


def _params(sem, vmem=VMEM_LIMIT):
    return pltpu.CompilerParams(dimension_semantics=sem, vmem_limit_bytes=vmem)


def _interleave(gens):
    results = [None] * len(gens)
    active = list(enumerate(gens))
    while active:
        still = []
        for i, g in active:
            try:
                next(g)
                still.append((i, g))
            except StopIteration as done:
                results[i] = done.value
        active = still
    return results


def _resident(shape):
    return pl.BlockSpec(shape, lambda *_: (0,) * len(shape), pipeline_mode=pl.Buffered(1))


def _mod_kernel(c_ref, w_ref, b_ref, o_ref):
    a_hi, a_lo = _split2(_silu(c_ref[...]))
    w_hi, w_lo = _split2(w_ref[0])
    acc = _bdot(a_hi, w_hi) + _bdot(a_lo, w_hi) + _bdot(a_hi, w_lo)
    o_ref[0] = acc + b_ref[0]


def _modulation(cvec, w_mod, b_mod):
    rows = cvec.shape[0]
    n = w_mod.shape[-1]
    return pl.pallas_call(
        _mod_kernel,
        grid=(DEPTH, n // MOD_TILE),
        in_specs=[
            pl.BlockSpec((rows, D_MODEL), lambda i, j: (0, 0)),
            pl.BlockSpec((1, D_MODEL, MOD_TILE), lambda i, j: (i, 0, j)),
            pl.BlockSpec((1, 1, MOD_TILE), lambda i, j: (i, 0, j)),
        ],
        out_specs=pl.BlockSpec((1, rows, MOD_TILE), lambda i, j: (i, 0, j)),
        out_shape=jax.ShapeDtypeStruct((DEPTH, rows, n), F32),
        compiler_params=_params(("arbitrary", "arbitrary")),
        name="modulation",
    )(cvec, w_mod, b_mod.reshape(DEPTH, 1, n))


class _Stream:
    def __init__(self, batch, seq, mod_row0, per_batch_mod):
        self.batch = batch
        self.seq = seq
        self.rows = batch * seq
        self.mod_row0 = mod_row0
        self.tokens_per_mod = seq if per_batch_mod else batch * seq

    def mod_spec(self, tile):
        row0, tpm = self.mod_row0, self.tokens_per_mod
        assert tpm % tile == 0
        return pl.BlockSpec((1, 3, D_MODEL), lambda t, *_: (row0 + (t * tile) // tpm, 0, 0))


def _row_groups(ref):
    return [pl.ds(s, SUB_ROWS) for s in range(0, ref.shape[0], SUB_ROWS)]


def _ffn_kernel(x_ref, m_ref, g_ref, wi_ref, wo_ref, o_ref, *, weight):
    m = m_ref[0]
    shift, scale, gate_m = m[0:1, :], 1.0 + m[1:2, :], weight * m[2:3, :]
    for rows in _row_groups(x_ref):
        x = x_ref[rows, :]
        h = (_rms(x, g_ref[0:1, :]) * scale + shift).astype(BF16)
        gate = _bdot(h, wi_ref[:, :D_FF])
        up = _bdot(h, wi_ref[:, D_FF:])
        act = (_silu(gate) * up).astype(BF16)
        out = _bdot(act, wo_ref[...])
        o_ref[rows, :] = x + gate_m * _rms(out, g_ref[1:2, :])


def _ffn(stream, y, m, g, w_in, w_out, layer, which, weight):
    tm = FFN_TILE
    return pl.pallas_call(
        functools.partial(_ffn_kernel, weight=weight),
        grid=(stream.rows // tm,),
        in_specs=[
            pl.BlockSpec((tm, D_MODEL), lambda t: (t, 0)),
            stream.mod_spec(tm),
            _resident((2, D_MODEL)),
            pl.BlockSpec((None, None, D_MODEL, 2 * D_FF), lambda t: (layer, which, 0, 0),
                         pipeline_mode=pl.Buffered(1)),
            pl.BlockSpec((None, None, D_FF, D_MODEL), lambda t: (layer, which, 0, 0),
                         pipeline_mode=pl.Buffered(1)),
        ],
        out_specs=pl.BlockSpec((tm, D_MODEL), lambda t: (t, 0)),
        out_shape=jax.ShapeDtypeStruct((stream.rows, D_MODEL), F32),
        compiler_params=_params(("parallel",)),
        name="ffn",
    )(y, m, g, w_in, w_out)


def _gla_gates(z, lbd):
    e = jnp.exp(-jnp.abs(z))
    r = 1.0 / (1.0 + e)
    pos = z >= 0.0
    sig = jnp.where(pos, r, e * r)
    sig_neg = jnp.where(pos, e * r, r)
    logf = jnp.maximum(jnp.log(lbd + (1.0 - lbd) * sig), LOGF_MIN)
    return logf, (1.0 - lbd) * sig_neg


def _proj_kernel(*refs, mode, has_bias, layer_idx):
    x_ref, m_ref, g_ref, w_ref = refs[:4]
    refs = refs[4:]
    b_ref = lb_ref = None
    if has_bias:
        b_ref, refs = refs[0], refs[1:]
    if mode == "hgrn":
        lb_ref, refs = refs[0], refs[1:]
    outs = refs
    m = m_ref[0]
    shift, scale = m[0:1, :], 1.0 + m[1:2, :]

    lower = []
    if mode == "hgrn":
        for d in range(2):
            lbp = lb_ref[d]
            e = jnp.exp(lbp - jnp.max(lbp, axis=0, keepdims=True))
            p = e / jnp.sum(e, axis=0, keepdims=True)
            lower.append(jnp.sum(p[1:layer_idx + 1, :], axis=0, keepdims=True))

    for rows in _row_groups(x_ref):
        h = (_rms(x_ref[rows, :], g_ref[0:1, :]) * scale + shift).astype(BF16)

        def part(p):
            r = _bdot(h, w_ref[:, p * D_MODEL:(p + 1) * D_MODEL])
            if has_bias:
                r = r + b_ref[:, p * D_MODEL:(p + 1) * D_MODEL]
            return r

        if mode == "split":
            for p, o_ref in enumerate(outs):
                o_ref[rows, :] = part(p)
        elif mode == "glu":
            a = part(0)
            outs[0][rows, :] = a * jax.nn.sigmoid(part(1))
        elif mode == "gated_pair":
            outs[0][rows, :] = part(0)
            outs[1][rows, :] = part(1) * part(2)
        else:
            outs[0][rows, :] = part(0)
            for d in range(2):
                logf, k = _gla_gates(part(1 + d), lower[d])
                outs[1 + 2 * d][rows, :] = logf
                outs[2 + 2 * d][rows, :] = k
            outs[5][rows, :] = part(3)
            outs[6][rows, :] = part(4)


def _proj(stream, y, m, g, w, b, mode, n_out, lb=None, layer_idx=0):
    tm = TOKEN_TILE
    n = w.shape[1]
    has_bias = b is not None
    in_specs = [
        pl.BlockSpec((tm, D_MODEL), lambda t: (t, 0)),
        stream.mod_spec(tm),
        _resident((2, D_MODEL)),
        _resident((D_MODEL, n)),
    ]
    args = [y, m, g, w]
    if has_bias:
        in_specs.append(_resident((1, n)))
        args.append(b.reshape(1, n))
    if mode == "hgrn":
        in_specs.append(_resident(lb.shape))
        args.append(lb)
    return pl.pallas_call(
        functools.partial(_proj_kernel, mode=mode, has_bias=has_bias, layer_idx=layer_idx),
        grid=(stream.rows // tm,),
        in_specs=in_specs,
        out_specs=[pl.BlockSpec((tm, D_MODEL), lambda t: (t, 0))] * n_out,
        out_shape=[jax.ShapeDtypeStruct((stream.rows, D_MODEL), F32)] * n_out,
        compiler_params=_params(("parallel",)),
        name="proj_" + mode,
    )(*args)


def _outproj_kernel(*refs, has_bias):
    mix_ref, w_ref = refs[:2]
    b_ref = refs[2] if has_bias else None
    x_ref, m_ref, g_ref, o_ref = refs[3:] if has_bias else refs[2:]
    gate_m = m_ref[0][2:3, :]
    for rows in _row_groups(x_ref):
        out = _bdot(mix_ref[rows, :], w_ref[...])
        if has_bias:
            out = out + b_ref[...]
        o_ref[rows, :] = x_ref[rows, :] + gate_m * _rms(out, g_ref[1:2, :])


def _outproj(stream, mix, w, b, y, m, g):
    tm = FFN_TILE
    has_bias = b is not None
    in_specs = [
        pl.BlockSpec((tm, D_MODEL), lambda t: (t, 0)),
        _resident((D_MODEL, D_MODEL)),
    ]
    args = [mix, w]
    if has_bias:
        in_specs.append(_resident((1, D_MODEL)))
        args.append(b.reshape(1, D_MODEL))
    in_specs += [
        pl.BlockSpec((tm, D_MODEL), lambda t: (t, 0)),
        stream.mod_spec(tm),
        _resident((2, D_MODEL)),
    ]
    args += [y, m, g]
    return pl.pallas_call(
        functools.partial(_outproj_kernel, has_bias=has_bias),
        grid=(stream.rows // tm,),
        in_specs=in_specs,
        out_specs=pl.BlockSpec((tm, D_MODEL), lambda t: (t, 0)),
        out_shape=jax.ShapeDtypeStruct((stream.rows, D_MODEL), F32),
        compiler_params=_params(("parallel",)),
        name="outproj",
    )(*args)


def _reduce_rows(x, op):
    r = x.shape[0]
    while r > SUBLANES:
        fold = next((f for f in (4, 2) if r % (f * SUBLANES) == 0), None)
        if fold is None:
            break
        x = op(x.reshape(fold, r // fold, x.shape[1]), axis=0)
        r //= fold
    return op(x, axis=0, keepdims=True)


def _rope(x, cos, sin_signed):
    lane = lax.broadcasted_iota(jnp.int32, x.shape, 1)
    first = (lane % (ATTN_HEAD_DIM // 2)) < (ATTN_HEAD_DIM // 4)
    rot = jnp.where(first, pltpu.roll(x, HEAD_W - ATTN_HEAD_DIM // 4, 1),
                    pltpu.roll(x, ATTN_HEAD_DIM // 4, 1))
    return x * cos + rot * sin_signed


def _attn_kernel(*refs, has_ctx, lam_init, seq, past, heads):
    if has_ctx:
        (q_ref, k_ref, v_ref, ck_ref, cv_ref, cosq_ref, sinq_ref, cosk_ref, sink_ref,
         lam_ref, g_ref, o_ref, k_scr, vt_scr) = refs
    else:
        q_ref, k_ref, v_ref, lam_ref, g_ref, o_ref, k_scr, vt_scr = refs

    def head_cols(h):
        return slice(h * HEAD_W, (h + 1) * HEAD_W)

    @pl.when(pl.program_id(2) == 0)
    def _():
        for h in range(heads):
            cols = head_cols(h)
            if has_ctx:
                k_scr[h, 0:past, :] = ck_ref[0, :, cols].astype(BF16)
                vt_scr[h, :, 0:past] = cv_ref[0, :, cols].T.astype(BF16)
                k_scr[h, past:past + seq, :] = _rope(k_ref[:, cols], cosk_ref[...], sink_ref[...]).astype(BF16)
                vt_scr[h, :, past:past + seq] = v_ref[:, cols].T.astype(BF16)
            else:
                k_scr[h] = k_ref[:, cols].astype(BF16)
                vt_scr[h] = v_ref[:, cols].T.astype(BF16)

    lp = lam_ref[...]
    lam = (jnp.exp(jnp.sum(lp[0:1, :] * lp[1:2, :], axis=-1, keepdims=True))
           - jnp.exp(jnp.sum(lp[2:3, :] * lp[3:4, :], axis=-1, keepdims=True)) + lam_init)

    n_keys = past + seq
    key_block = min(ATTN_KEY_BLOCK, n_keys)

    def softmax_pv(h, qm):
        def scores(j):
            return _bdot_nt(k_scr[h, j:j + key_block, :], qm)

        m = l = acc = None
        s_next = scores(0)
        for j in range(0, n_keys, key_block):
            s = s_next
            if j + key_block < n_keys:
                s_next = scores(j + key_block)
            yield
            mj = _reduce_rows(s, jnp.max)
            m_new = mj if m is None else jnp.maximum(m, mj)
            p = jnp.exp(s - m_new)
            lj = _reduce_rows(p, jnp.sum)
            yield
            pv = _bdot(vt_scr[h, :, j:j + key_block], p.astype(BF16))
            if m is None:
                l, acc = lj, pv
            else:
                alpha = jnp.exp(m - m_new)
                l, acc = alpha * l + lj, alpha * acc + pv
            m = m_new
            yield
        return acc / l

    for h0 in range(0, heads, ATTN_HEAD_GROUP):
        group = range(h0, min(h0 + ATTN_HEAD_GROUP, heads))
        branches = []
        for h in group:
            q = q_ref[:, head_cols(h)]
            if has_ctx:
                q = _rope(q, cosq_ref[...], sinq_ref[...])
            q = q * (ATTN_HEAD_DIM ** -0.5)
            lane = lax.broadcasted_iota(jnp.int32, q.shape, 1)
            branches.append(softmax_pv(h, jnp.where(lane < ATTN_HEAD_DIM, q, 0.0).astype(BF16)))
            branches.append(softmax_pv(h, jnp.where(lane >= ATTN_HEAD_DIM, q, 0.0).astype(BF16)))
        outs = _interleave(branches)
        for n, h in enumerate(group):
            ot = outs[2 * n] - lam * outs[2 * n + 1]
            o_ref[:, head_cols(h)] = (_rms(ot.T, g_ref[...]) * (1.0 - lam_init)).astype(BF16)


def _attention(stream, q, k, v, lam_params, subln_g, layer_idx, heads, ctx=None):
    seq, tq = stream.seq, ATTN_Q_TILE
    nq = seq // tq
    width = heads * HEAD_W
    lam_init = 0.8 - 0.6 * math.exp(-0.3 * layer_idx)
    has_ctx = ctx is not None
    in_specs = [
        pl.BlockSpec((tq, width), lambda b, h, i: (b * nq + i, h)),
        pl.BlockSpec((seq, width), lambda b, h, i: (b, h)),
        pl.BlockSpec((seq, width), lambda b, h, i: (b, h)),
    ]
    args = [q, k, v]
    past = 0
    if has_ctx:
        ck, cv, cache_layer, past, cos, sin_signed = ctx
        in_specs += [
            pl.BlockSpec((1, past, width), lambda b, h, i: (b, cache_layer, h)),
            pl.BlockSpec((1, past, width), lambda b, h, i: (b, cache_layer, h)),
            pl.BlockSpec((tq, HEAD_W), lambda b, h, i: (i, 0)),
            pl.BlockSpec((tq, HEAD_W), lambda b, h, i: (i, 0)),
            _resident((seq, HEAD_W)),
            _resident((seq, HEAD_W)),
        ]
        args += [ck, cv, cos, sin_signed, cos, sin_signed]
    in_specs += [_resident((4, ATTN_HEAD_DIM)), _resident((1, HEAD_W))]
    args += [lam_params, subln_g.reshape(1, HEAD_W)]
    return pl.pallas_call(
        functools.partial(_attn_kernel, has_ctx=has_ctx, lam_init=lam_init, seq=seq, past=past, heads=heads),
        grid=(stream.batch, ATTN_HEADS // heads, nq),
        in_specs=in_specs,
        out_specs=pl.BlockSpec((tq, width), lambda b, h, i: (b * nq + i, h)),
        out_shape=jax.ShapeDtypeStruct((stream.rows, D_MODEL), BF16),
        scratch_shapes=[pltpu.VMEM((heads, past + seq, HEAD_W), BF16),
                        pltpu.VMEM((heads, HEAD_W, past + seq), BF16)],
        compiler_params=_params(("parallel", "parallel", "arbitrary")),
        name="diff_attention",
    )(*args)


def _rope_tables(n_tokens):
    rows = n_tokens // GRID_W
    row = jnp.repeat(jnp.arange(rows), GRID_W).astype(F32)
    col = jnp.tile(jnp.arange(GRID_W), rows).astype(F32)
    n_freq = ATTN_HEAD_DIM // 4
    inv_freq = ROPE_BASE ** (-jnp.arange(n_freq, dtype=F32) / n_freq)
    ang_r = row[:, None] * inv_freq
    ang_c = col[:, None] * inv_freq
    ang = jnp.concatenate([ang_r, ang_r, ang_c, ang_c], axis=-1)
    sign = jnp.concatenate([-jnp.ones((n_freq,), F32), jnp.ones((n_freq,), F32)] * 2)
    cos = jnp.cos(ang)
    sin_signed = jnp.sin(ang) * sign
    return jnp.tile(cos, (1, 2)), jnp.tile(sin_signed, (1, 2))


def _boundary_rows(b, m, fwd):
    c = b.shape[0]
    idx = m - 1 if fwd else m
    if 2 * m >= SUBLANES:
        b3 = b.reshape(c // (2 * m), 2 * m, HGRN_DK)
        return jnp.broadcast_to(b3[:, idx:idx + 1, :], b3.shape).reshape(c, HGRN_DK)
    b3 = b.reshape(c // SUBLANES, SUBLANES, HGRN_DK)
    trow = lax.broadcasted_iota(jnp.int32, b3.shape, 1)
    br = jnp.broadcast_to(b3[:, idx:idx + 1, :], b3.shape)
    for start in range(2 * m, SUBLANES, 2 * m):
        br = jnp.where(trow >= start, b3[:, start + idx:start + idx + 1, :], br)
    return br.reshape(c, HGRN_DK)


def _gla_chunk(q, k, v, logf, st, tri, level, fwd):
    c = q.shape[0]
    hi, mid, lo = _split3(logf)
    yield
    b = _bdot(tri, hi) + _bdot(tri, mid) + _bdot(tri, lo)
    yield
    b_last = b[c - 1:c, :] if fwd else b[0:1, :]
    qe = (q * jnp.exp(b)).astype(BF16)
    ke = (k * jnp.exp(b_last - b)).astype(BF16)
    yield
    o = _bdot_nt(qe, st.astype(BF16))
    st_new = st * jnp.exp(b_last) + _bdot(v.T.astype(BF16), ke)
    yield

    row = lax.broadcasted_iota(jnp.int32, (c, HGRN_DK), 0)
    sc = jnp.where(level == -1, jnp.sum(q * k, axis=-1, keepdims=True), 0.0)
    pending = None
    for lg in range(int(math.log2(c))):
        m = 2 ** lg
        ex = jnp.exp(-jnp.abs(b - _boundary_rows(b, m, fwd)))
        later = (row & m) != 0
        z = (jnp.where(later if fwd else jnp.logical_not(later), q, k) * ex).astype(BF16)
        yield
        if pending is not None:
            sc = jnp.where(level == pending[0], pending[1], sc)
        pending = (lg, _bdot_nt(z, z))
    yield
    sc = jnp.where(level == pending[0], pending[1], sc)
    return o + _bdot(sc.astype(BF16), v.astype(BF16)), st_new


def _hgrn_kernel(*refs, has_state, emit_state, seq):
    q_ref, lff_ref, kf_ref, lfb_ref, kb_ref, i_ref, g_ref, ng_ref = refs[:8]
    rest = list(refs[8:])
    s0_refs = [rest.pop(0), rest.pop(0)] if has_state else [None, None]
    o_ref = rest.pop(0)
    sfin_ref = rest.pop(0) if emit_state else None
    of_scr, ob_scr, stf_scr, stb_scr = rest
    c = GLA_CHUNK
    n_chunks = seq // c

    ti = lax.broadcasted_iota(jnp.int32, (c, c), 0)
    si = lax.broadcasted_iota(jnp.int32, (c, c), 1)
    parted = 31 - lax.clz(ti ^ si)
    dirs = (
        (True, lff_ref, kf_ref, stf_scr, of_scr, jnp.where(si <= ti, 1.0, 0.0).astype(BF16),
         jnp.where(si <= ti, parted, -2)),
        (False, lfb_ref, kb_ref, stb_scr, ob_scr, jnp.where(si >= ti, 1.0, 0.0).astype(BF16),
         jnp.where(si >= ti, parted, -2)),
    )
    for h in range(GLA_HEADS):
        for d in range(2):
            dirs[d][3][h] = s0_refs[d][h].T if has_state else jnp.zeros((HGRN_DV, HGRN_DK), F32)

    def body(i, carry):
        chains, sinks = [], []
        for h in range(GLA_HEADS):
            cols = slice(h * HGRN_DK, (h + 1) * HGRN_DK)
            for fwd, lf_ref, k_ref, st_scr, o_scr, tri, level in dirs:
                ci = i if fwd else n_chunks - 1 - i
                rows = pl.ds(pl.multiple_of(ci * c, c), c)
                chains.append(_gla_chunk(q_ref[rows, cols], k_ref[rows, cols], i_ref[rows, cols],
                                         lf_ref[rows, cols], st_scr[h], tri, level, fwd))
                sinks.append((st_scr, o_scr, h, rows, cols))
        for (o, st_new), (st_scr, o_scr, h, rows, cols) in zip(_interleave(chains), sinks):
            st_scr[h] = st_new
            o_scr[rows, cols] = o
        return carry

    lax.fori_loop(0, n_chunks, body, 0)
    for h in range(GLA_HEADS):
        cols = slice(h * HGRN_DK, (h + 1) * HGRN_DK)
        if emit_state:
            sfin_ref[0, h] = stf_scr[h].T
            sfin_ref[1, h] = stb_scr[h].T
        o_ref[:, cols] = (_rms(of_scr[:, cols] + ob_scr[:, cols], ng_ref[...])
                          * _silu(g_ref[:, cols])).astype(BF16)


def _hgrn(stream, parts, norm_g, state=None, state_layer=0, emit_state=False):
    seq = stream.seq
    width = GLA_HEADS * HGRN_DK
    has_state = state is not None
    col = pl.BlockSpec((seq, width), lambda b, h: (b, h))
    in_specs = [col] * 7 + [_resident((1, HGRN_DV))]
    args = list(parts) + [norm_g.reshape(1, HGRN_DV)]
    if has_state:
        for d in range(2):
            in_specs.append(pl.BlockSpec((None, None, None, GLA_HEADS, HGRN_DK, HGRN_DV),
                                         lambda b, h, d=d: (b, state_layer, d, h, 0, 0)))
            args.append(state)
    out_specs = [pl.BlockSpec((seq, width), lambda b, h: (b, h))]
    out_shape = [jax.ShapeDtypeStruct((stream.rows, D_MODEL), BF16)]
    if emit_state:
        out_specs.append(pl.BlockSpec((None, 2, GLA_HEADS, HGRN_DK, HGRN_DV), lambda b, h: (b, 0, h, 0, 0)))
        out_shape.append(jax.ShapeDtypeStruct((stream.batch, 2, HGRN_HEADS, HGRN_DK, HGRN_DV), F32))
    res = pl.pallas_call(
        functools.partial(_hgrn_kernel, has_state=has_state, emit_state=emit_state, seq=seq),
        grid=(stream.batch, HGRN_HEADS // GLA_HEADS),
        in_specs=in_specs,
        out_specs=out_specs,
        out_shape=out_shape,
        scratch_shapes=[pltpu.VMEM((seq, width), F32), pltpu.VMEM((seq, width), F32),
                        pltpu.VMEM((GLA_HEADS, HGRN_DV, HGRN_DK), F32),
                        pltpu.VMEM((GLA_HEADS, HGRN_DV, HGRN_DK), F32)],
        compiler_params=_params(("parallel", "parallel")),
        name="hgrn2",
    )(*args)
    return res if emit_state else (res[0], None)


def _fill_padded(pad_scr, src_ref, seq):
    zeros = jnp.zeros((CONV_HALO, D_MODEL), F32)
    pad_scr[0:CONV_HALO, :] = zeros
    pad_scr[CONV_HALO:CONV_HALO + seq, :] = src_ref[...]
    pad_scr[CONV_HALO + seq:2 * CONV_HALO + seq, :] = zeros


def _dwconv_tile(pad_scr, w_ref, width):
    tile = CONV_TILE
    base = pl.program_id(1) * tile
    first = CONV_HALO - width // 2
    acc = None
    for r in range(SUBLANES):
        part = None
        for j in range(width):
            if (first + j) % SUBLANES != r:
                continue
            start = pl.multiple_of(base + (first + j - r), SUBLANES)
            term = pad_scr[pl.ds(start, tile + SUBLANES), :] * w_ref[j:j + 1, :]
            part = term if part is None else part + term
        if part is not None:
            shifted = part[r:r + tile, :]
            acc = shifted if acc is None else acc + shifted
    return acc


def _conformer_conv_kernel(u_ref, w_ref, bdw_ref, lng_ref, lnb_ref, o_ref, pad_scr, *, seq):
    @pl.when(pl.program_id(1) == 0)
    def _():
        _fill_padded(pad_scr, u_ref, seq)

    u = _dwconv_tile(pad_scr, w_ref, CONV_WIDTH) + bdw_ref[...]
    xc = u - jnp.mean(u, axis=-1, keepdims=True)
    y = xc * lax.rsqrt(jnp.mean(xc * xc, axis=-1, keepdims=True) + EPS) * lng_ref[...] + lnb_ref[...]
    o_ref[...] = _silu(y).astype(BF16)


def _short_conv_kernel(p_ref, bg_ref, w_ref, o_ref, pad_scr, *, seq):
    @pl.when(pl.program_id(1) == 0)
    def _():
        _fill_padded(pad_scr, p_ref, seq)

    o_ref[...] = (bg_ref[...] * _dwconv_tile(pad_scr, w_ref, SHORT_CONV_WIDTH)).astype(BF16)


def _conv_call(stream, body, seq_inputs, tile_inputs, small_inputs):
    seq, tile = stream.seq, CONV_TILE
    nt = seq // tile
    in_specs = ([pl.BlockSpec((seq, D_MODEL), lambda b, t: (b, 0))] * len(seq_inputs)
                + [pl.BlockSpec((tile, D_MODEL), lambda b, t: (b * nt + t, 0))] * len(tile_inputs)
                + [_resident(a.shape) for a in small_inputs])
    return pl.pallas_call(
        functools.partial(body, seq=seq),
        grid=(stream.batch, nt),
        in_specs=in_specs,
        out_specs=pl.BlockSpec((tile, D_MODEL), lambda b, t: (b * nt + t, 0)),
        out_shape=jax.ShapeDtypeStruct((stream.rows, D_MODEL), BF16),
        scratch_shapes=[pltpu.VMEM((seq + 2 * CONV_HALO, D_MODEL), F32)],
        compiler_params=_params(("parallel", "arbitrary")),
        name="dwconv",
    )(*seq_inputs, *tile_inputs, *small_inputs)


def kernel(x_prompt, x_sample, c, cache_k, cache_v, state_hgrn, c_ctx, w_mod, b_mod, norm_g,
           w_ffn_in, w_ffn_out, w_attn_qkv, w_attn_o, attn_lambda, attn_subln_g,
           w_hgrn_in, w_hgrn_o, hgrn_norm_g, hgrn_lb, w_cm_in, b_cm_in, w_cm_dw, b_cm_dw,
           cm_ln_g, cm_ln_b, w_cm_out, b_cm_out, w_sc_in, w_sc_conv, w_sc_out):
    batch, seq, _ = x_prompt.shape
    dec_batch, dec_seq, _ = x_sample.shape
    past = cache_k.shape[2]
    prompt = _Stream(batch, seq, 0, per_batch_mod=False)
    sample = _Stream(dec_batch, dec_seq, 1, per_batch_mod=True)
    streams = (prompt, sample)

    cvec = jnp.concatenate([c_ctx[None], c, jnp.zeros((SUBLANES - 1 - dec_batch, D_MODEL), F32)], axis=0)
    mods = _modulation(cvec, w_mod, b_mod).reshape(DEPTH, SUBLANES, 3, 3, D_MODEL)

    w_ffn_in_b = w_ffn_in.astype(BF16)
    w_ffn_out_b = w_ffn_out.astype(BF16)
    rope = _rope_tables(dec_seq)

    ys = [x_prompt.reshape(prompt.rows, D_MODEL), x_sample.reshape(sample.rows, D_MODEL)]
    new_k = new_v = new_s = None
    for i in range(DEPTH):
        kind, j = i % N_MIXERS, i // N_MIXERS
        g = norm_g[i]

        def sub_mod(s):
            return mods[i, :, s]

        for si, st in enumerate(streams):
            ys[si] = _ffn(st, ys[si], sub_mod(0), g[0], w_ffn_in_b, w_ffn_out_b, i, 0, 0.5)

        m1 = sub_mod(1)
        for si, st in enumerate(streams):
            y = ys[si]
            is_prompt = si == 0
            bias_out = None
            if kind == 0:
                q, k, v = _proj(st, y, m1, g[1], w_attn_qkv[j].astype(BF16), None, "split", 3)
                if is_prompt:
                    new_k = k.reshape(batch, 1, seq, ATTN_HEADS, HEAD_W)
                    new_v = v.reshape(batch, 1, seq, ATTN_HEADS, HEAD_W)
                    mix = _attention(st, q, k, v, attn_lambda[j], attn_subln_g[j], i, ATTN_HEADS)
                else:
                    ctx = (cache_k.reshape(dec_batch, -1, D_MODEL), cache_v.reshape(dec_batch, -1, D_MODEL),
                           j, past, rope[0], rope[1])
                    mix = _attention(st, q, k, v, attn_lambda[j], attn_subln_g[j], i, ATTN_HEAD_GROUP, ctx)
                w_o = w_attn_o[j]
            elif kind == 1:
                parts = _proj(st, y, m1, g[1], w_hgrn_in[j].astype(BF16), None, "hgrn", 7,
                              lb=hgrn_lb, layer_idx=i)
                if is_prompt:
                    mix, s_fin = _hgrn(st, parts, hgrn_norm_g[j], emit_state=True)
                    new_s = s_fin[:, None]
                else:
                    mix, _ = _hgrn(st, parts, hgrn_norm_g[j], state=state_hgrn, state_layer=j)
                w_o = w_hgrn_o[j]
            elif kind == 2:
                (u,) = _proj(st, y, m1, g[1], w_cm_in[j].astype(BF16), b_cm_in[j], "glu", 1)
                mix = _conv_call(st, _conformer_conv_kernel, [u], [],
                                 [w_cm_dw[j], b_cm_dw[j].reshape(1, D_MODEL),
                                  cm_ln_g[j].reshape(1, D_MODEL), cm_ln_b[j].reshape(1, D_MODEL)])
                w_o = w_cm_out[j]
                bias_out = b_cm_out[j]
            else:
                bg, prod = _proj(st, y, m1, g[1], w_sc_in[j].astype(BF16), None, "gated_pair", 2)
                mix = _conv_call(st, _short_conv_kernel, [prod], [bg], [w_sc_conv[j]])
                w_o = w_sc_out[j]
            ys[si] = _outproj(st, mix, w_o.astype(BF16), bias_out, y, m1, g[1])

        for si, st in enumerate(streams):
            ys[si] = _ffn(st, ys[si], sub_mod(2), g[2], w_ffn_in_b, w_ffn_out_b, i, 1, 0.5)

    return (ys[0].reshape(batch, seq, D_MODEL), ys[1].reshape(dec_batch, dec_seq, D_MODEL),
            new_k, new_v, new_s)
```

```python
import functools
import math

import jax
import jax.numpy as jnp
from jax import lax
from jax.experimental import pallas as pl
from jax.experimental.pallas import tpu as pltpu

D_MODEL = 1024
DEPTH = 4
GRID_W = 64
N_MIXERS = 4
ATTN_HEAD_DIM = 64
ATTN_HEADS = D_MODEL // (2 * ATTN_HEAD_DIM)
HEAD_W = 2 * ATTN_HEAD_DIM
ROPE_BASE = 10000.0
HGRN_DK = 128
HGRN_HEADS = D_MODEL // HGRN_DK
HGRN_DV = D_MODEL // HGRN_HEADS
CONV_WIDTH = 31
SHORT_CONV_WIDTH = 3
D_FF = ((8 * D_MODEL // 3 + 127) // 128) * 128
EPS = 1e-6

F32 = jnp.float32
BF16 = jnp.bfloat16

SUBLANES = 8
FFN_TILE = 1024
TOKEN_TILE = 512
SUB_ROWS = 256
ATTN_Q_TILE = 512
ATTN_KEY_BLOCK = 512
ATTN_HEAD_GROUP = 2
GLA_CHUNK = 128
CONV_TILE = 256
CONV_HALO = 16
MOD_TILE = 2304
VMEM_LIMIT = 56 * 1024 * 1024
FFN_VMEM_LIMIT = 60 * 1024 * 1024
LOGF_MIN = -180.0
GLA_HEADS = 2


def _bdot(a, b):
    return jnp.dot(a, b, preferred_element_type=F32)


def _bdot_nt(a, b):
    return lax.dot_general(a, b, (((1,), (1,)), ((), ())), preferred_element_type=F32)


def _rms(x, g):
    return x * lax.rsqrt(jnp.mean(x * x, axis=-1, keepdims=True) + EPS) * g


def _silu(x):
    return x * jax.nn.sigmoid(x)


def _split2(x):
    hi = x.astype(BF16)
    lo = (x - hi.astype(F32)).astype(BF16)
    return hi, lo


def _split3(x):
    hi = x.astype(BF16)
    r = x - hi.astype(F32)
    mid = r.astype(BF16)
    lo = (r - mid.astype(F32)).astype(BF16)
    return hi, mid, lo


def _params(sem, vmem=VMEM_LIMIT):
    return pltpu.CompilerParams(dimension_semantics=sem, vmem_limit_bytes=vmem)


def _interleave(gens):
    results = [None] * len(gens)
    active = list(enumerate(gens))
    while active:
        still = []
        for i, g in active:
            try:
                next(g)
                still.append((i, g))
            except StopIteration as done:
                results[i] = done.value
        active = still
    return results


def _resident(shape):
    return pl.BlockSpec(shape, lambda *_: (0,) * len(shape), pipeline_mode=pl.Buffered(1))


def _mod_kernel(c_ref, w_ref, b_ref, o_ref):
    a_hi, a_lo = _split2(_silu(c_ref[...]))
    w_hi, w_lo = _split2(w_ref[0])
    acc = _bdot(a_hi, w_hi) + _bdot(a_lo, w_hi) + _bdot(a_hi, w_lo)
    o_ref[0] = acc + b_ref[0]


def _modulation(cvec, w_mod, b_mod):
    rows = cvec.shape[0]
    n = w_mod.shape[-1]
    return pl.pallas_call(
        _mod_kernel,
        grid=(DEPTH, n // MOD_TILE),
        in_specs=[
            pl.BlockSpec((rows, D_MODEL), lambda i, j: (0, 0)),
            pl.BlockSpec((1, D_MODEL, MOD_TILE), lambda i, j: (i, 0, j)),
            pl.BlockSpec((1, 1, MOD_TILE), lambda i, j: (i, 0, j)),
        ],
        out_specs=pl.BlockSpec((1, rows, MOD_TILE), lambda i, j: (i, 0, j)),
        out_shape=jax.ShapeDtypeStruct((DEPTH, rows, n), F32),
        compiler_params=_params(("arbitrary", "arbitrary")),
        name="modulation",
    )(cvec, w_mod, b_mod.reshape(DEPTH, 1, n))


class _Stream:
    def __init__(self, batch, seq, mod_row0, per_batch_mod):
        self.batch = batch
        self.seq = seq
        self.rows = batch * seq
        self.mod_row0 = mod_row0
        self.tokens_per_mod = seq if per_batch_mod else batch * seq

    def mod_spec(self, tile):
        row0, tpm = self.mod_row0, self.tokens_per_mod
        assert tpm % tile == 0
        return pl.BlockSpec((1, 3, D_MODEL), lambda t, *_: (row0 + (t * tile) // tpm, 0, 0))


def _row_groups(ref):
    return [pl.ds(s, SUB_ROWS) for s in range(0, ref.shape[0], SUB_ROWS)]


def _ffn_kernel(*refs, weight, has_mixer, mixer_bias, has_cast):
    it = iter(refs)
    x_ref = next(it)
    if has_mixer:
        mix_ref, wmix_ref = next(it), next(it)
        bmix_ref = next(it) if mixer_bias else None
        mm_ref, gm_ref = next(it), next(it)
    m_ref, g_ref, wi_ref, wo_ref = next(it), next(it), next(it), next(it)
    cast_src = next(it) if has_cast else None
    o_ref = next(it)
    if has_cast:
        next(it)[...] = cast_src[...].astype(BF16)

    m = m_ref[0]
    shift, scale, gate_m = m[0:1, :], 1.0 + m[1:2, :], weight * m[2:3, :]
    for rows in _row_groups(x_ref):
        x = x_ref[rows, :]
        if has_mixer:
            mixed = _bdot(mix_ref[rows, :], wmix_ref[...])
            if mixer_bias:
                mixed = mixed + bmix_ref[...]
            x = x + mm_ref[0][2:3, :] * _rms(mixed, gm_ref[1:2, :])
        h = (_rms(x, g_ref[0:1, :]) * scale + shift).astype(BF16)
        gate = _bdot(h, wi_ref[:, :D_FF])
        up = _bdot(h, wi_ref[:, D_FF:])
        act = (_silu(gate) * up).astype(BF16)
        out = _bdot(act, wo_ref[...])
        o_ref[rows, :] = x + gate_m * _rms(out, g_ref[1:2, :])


def _ffn(stream, y, m, g, w_in, w_out, weight, mixer=None, cast=None):
    tm = FFN_TILE
    steps = stream.rows // tm
    in_specs = [pl.BlockSpec((tm, D_MODEL), lambda t: (t, 0))]
    args = [y]
    mixer_bias = False
    if mixer is not None:
        mix, w_mix, b_mix, m_mix, g_mix = mixer
        mixer_bias = b_mix is not None
        in_specs += [pl.BlockSpec((tm, D_MODEL), lambda t: (t, 0)), _resident((D_MODEL, D_MODEL))]
        args += [mix, w_mix]
        if mixer_bias:
            in_specs.append(_resident((1, D_MODEL)))
            args.append(b_mix.reshape(1, D_MODEL))
        in_specs += [stream.mod_spec(tm), _resident((2, D_MODEL))]
        args += [m_mix, g_mix]
    in_specs += [stream.mod_spec(tm), _resident((2, D_MODEL)),
                 _resident((D_MODEL, 2 * D_FF)), _resident((D_FF, D_MODEL))]
    args += [m, g, w_in, w_out]
    out_specs = [pl.BlockSpec((tm, D_MODEL), lambda t: (t, 0))]
    out_shape = [jax.ShapeDtypeStruct((stream.rows, D_MODEL), F32)]
    if cast is not None:
        src, layer, which = cast
        r, c = src.shape[2:]
        in_specs.append(pl.BlockSpec((None, None, r // steps, c), lambda t: (layer, which, t, 0)))
        args.append(src)
        out_specs.append(pl.BlockSpec((r // steps, c), lambda t: (t, 0)))
        out_shape.append(jax.ShapeDtypeStruct((r, c), BF16))
    res = pl.pallas_call(
        functools.partial(_ffn_kernel, weight=weight, has_mixer=mixer is not None,
                          mixer_bias=mixer_bias, has_cast=cast is not None),
        grid=(steps,),
        in_specs=in_specs,
        out_specs=out_specs,
        out_shape=out_shape,
        compiler_params=_params(("parallel",), FFN_VMEM_LIMIT),
        name="ffn",
    )(*args)
    return (res[0], res[1]) if cast is not None else (res[0], None)


def _gla_gates(z, lbd):
    e = jnp.exp(-jnp.abs(z))
    r = 1.0 / (1.0 + e)
    pos = z >= 0.0
    sig = jnp.where(pos, r, e * r)
    sig_neg = jnp.where(pos, e * r, r)
    logf = jnp.maximum(jnp.log(lbd + (1.0 - lbd) * sig), LOGF_MIN)
    return logf, (1.0 - lbd) * sig_neg


def _proj_kernel(*refs, mode, has_bias, layer_idx):
    x_ref, m_ref, g_ref, w_ref = refs[:4]
    refs = refs[4:]
    b_ref = lb_ref = None
    if has_bias:
        b_ref, refs = refs[0], refs[1:]
    if mode == "hgrn":
        lb_ref, refs = refs[0], refs[1:]
    outs = refs
    m = m_ref[0]
    shift, scale = m[0:1, :], 1.0 + m[1:2, :]

    lower = []
    if mode == "hgrn":
        for d in range(2):
            lbp = lb_ref[d]
            e = jnp.exp(lbp - jnp.max(lbp, axis=0, keepdims=True))
            p = e / jnp.sum(e, axis=0, keepdims=True)
            lower.append(jnp.sum(p[1:layer_idx + 1, :], axis=0, keepdims=True))

    for rows in _row_groups(x_ref):
        h = (_rms(x_ref[rows, :], g_ref[0:1, :]) * scale + shift).astype(BF16)

        def part(p):
            r = _bdot(h, w_ref[:, p * D_MODEL:(p + 1) * D_MODEL])
            if has_bias:
                r = r + b_ref[:, p * D_MODEL:(p + 1) * D_MODEL]
            return r

        if mode == "split":
            for p, o_ref in enumerate(outs):
                o_ref[rows, :] = part(p)
        elif mode == "glu":
            a = part(0)
            outs[0][rows, :] = a * jax.nn.sigmoid(part(1))
        elif mode == "gated_pair":
            outs[0][rows, :] = part(0)
            outs[1][rows, :] = part(1) * part(2)
        else:
            outs[0][rows, :] = part(0)
            for d in range(2):
                logf, k = _gla_gates(part(1 + d), lower[d])
                outs[1 + 2 * d][rows, :] = logf
                outs[2 + 2 * d][rows, :] = k
            outs[5][rows, :] = part(3)
            outs[6][rows, :] = part(4)


def _proj(stream, y, m, g, w, b, mode, n_out, lb=None, layer_idx=0):
    tm = TOKEN_TILE
    n = w.shape[1]
    has_bias = b is not None
    in_specs = [
        pl.BlockSpec((tm, D_MODEL), lambda t: (t, 0)),
        stream.mod_spec(tm),
        _resident((2, D_MODEL)),
        _resident((D_MODEL, n)),
    ]
    args = [y, m, g, w]
    if has_bias:
        in_specs.append(_resident((1, n)))
        args.append(b.reshape(1, n))
    if mode == "hgrn":
        in_specs.append(_resident(lb.shape))
        args.append(lb)
    return pl.pallas_call(
        functools.partial(_proj_kernel, mode=mode, has_bias=has_bias, layer_idx=layer_idx),
        grid=(stream.rows // tm,),
        in_specs=in_specs,
        out_specs=[pl.BlockSpec((tm, D_MODEL), lambda t: (t, 0))] * n_out,
        out_shape=[jax.ShapeDtypeStruct((stream.rows, D_MODEL), F32)] * n_out,
        compiler_params=_params(("parallel",)),
        name="proj_" + mode,
    )(*args)


def _reduce_rows(x, op):
    r = x.shape[0]
    while r > SUBLANES:
        fold = next((f for f in (4, 2) if r % (f * SUBLANES) == 0), None)
        if fold is None:
            break
        x = op(x.reshape(fold, r // fold, x.shape[1]), axis=0)
        r //= fold
    return op(x, axis=0, keepdims=True)


def _rope(x, cos, sin_signed):
    lane = lax.broadcasted_iota(jnp.int32, x.shape, 1)
    first = (lane % (ATTN_HEAD_DIM // 2)) < (ATTN_HEAD_DIM // 4)
    rot = jnp.where(first, pltpu.roll(x, HEAD_W - ATTN_HEAD_DIM // 4, 1),
                    pltpu.roll(x, ATTN_HEAD_DIM // 4, 1))
    return x * cos + rot * sin_signed


def _attn_kernel(*refs, has_ctx, lam_init, seq, past, heads):
    if has_ctx:
        (q_ref, k_ref, v_ref, ck_ref, cv_ref, cosq_ref, sinq_ref, cosk_ref, sink_ref,
         lam_ref, g_ref, o_ref, k_scr, vt_scr) = refs
    else:
        q_ref, k_ref, v_ref, lam_ref, g_ref, o_ref, k_scr, vt_scr = refs

    def head_cols(h):
        return slice(h * HEAD_W, (h + 1) * HEAD_W)

    @pl.when(pl.program_id(2) == 0)
    def _():
        for h in range(heads):
            cols = head_cols(h)
            if has_ctx:
                k_scr[h, 0:past, :] = ck_ref[0, :, cols].astype(BF16)
                vt_scr[h, :, 0:past] = cv_ref[0, :, cols].T.astype(BF16)
                k_scr[h, past:past + seq, :] = _rope(k_ref[:, cols], cosk_ref[...], sink_ref[...]).astype(BF16)
                vt_scr[h, :, past:past + seq] = v_ref[:, cols].T.astype(BF16)
            else:
                k_scr[h] = k_ref[:, cols].astype(BF16)
                vt_scr[h] = v_ref[:, cols].T.astype(BF16)

    lp = lam_ref[...]
    lam = (jnp.exp(jnp.sum(lp[0:1, :] * lp[1:2, :], axis=-1, keepdims=True))
           - jnp.exp(jnp.sum(lp[2:3, :] * lp[3:4, :], axis=-1, keepdims=True)) + lam_init)

    n_keys = past + seq
    key_block = min(ATTN_KEY_BLOCK, n_keys)

    def softmax_pv(h, qm):
        def scores(j):
            return _bdot_nt(k_scr[h, j:j + key_block, :], qm)

        m = l = acc = None
        s_next = scores(0)
        for j in range(0, n_keys, key_block):
            s = s_next
            if j + key_block < n_keys:
                s_next = scores(j + key_block)
            yield
            mj = _reduce_rows(s, jnp.max)
            m_new = mj if m is None else jnp.maximum(m, mj)
            p = jnp.exp(s - m_new)
            lj = _reduce_rows(p, jnp.sum)
            yield
            pv = _bdot(vt_scr[h, :, j:j + key_block], p.astype(BF16))
            if m is None:
                l, acc = lj, pv
            else:
                alpha = jnp.exp(m - m_new)
                l, acc = alpha * l + lj, alpha * acc + pv
            m = m_new
            yield
        return acc / l

    for h0 in range(0, heads, ATTN_HEAD_GROUP):
        group = range(h0, min(h0 + ATTN_HEAD_GROUP, heads))
        branches = []
        for h in group:
            q = q_ref[:, head_cols(h)]
            if has_ctx:
                q = _rope(q, cosq_ref[...], sinq_ref[...])
            q = q * (ATTN_HEAD_DIM ** -0.5)
            lane = lax.broadcasted_iota(jnp.int32, q.shape, 1)
            branches.append(softmax_pv(h, jnp.where(lane < ATTN_HEAD_DIM, q, 0.0).astype(BF16)))
            branches.append(softmax_pv(h, jnp.where(lane >= ATTN_HEAD_DIM, q, 0.0).astype(BF16)))
        outs = _interleave(branches)
        for n, h in enumerate(group):
            ot = outs[2 * n] - lam * outs[2 * n + 1]
            o_ref[:, head_cols(h)] = (_rms(ot.T, g_ref[...]) * (1.0 - lam_init)).astype(BF16)


def _attention(stream, q, k, v, lam_params, subln_g, layer_idx, heads, ctx=None):
    seq, tq = stream.seq, min(ATTN_Q_TILE, stream.seq)
    nq = seq // tq
    width = heads * HEAD_W
    lam_init = 0.8 - 0.6 * math.exp(-0.3 * layer_idx)
    has_ctx = ctx is not None
    in_specs = [
        pl.BlockSpec((tq, width), lambda b, h, i: (b * nq + i, h)),
        pl.BlockSpec((seq, width), lambda b, h, i: (b, h)),
        pl.BlockSpec((seq, width), lambda b, h, i: (b, h)),
    ]
    args = [q, k, v]
    past = 0
    if has_ctx:
        ck, cv, cache_layer, past, cos, sin_signed = ctx
        in_specs += [
            pl.BlockSpec((1, past, width), lambda b, h, i: (b, cache_layer, h)),
            pl.BlockSpec((1, past, width), lambda b, h, i: (b, cache_layer, h)),
            pl.BlockSpec((tq, HEAD_W), lambda b, h, i: (i, 0)),
            pl.BlockSpec((tq, HEAD_W), lambda b, h, i: (i, 0)),
            _resident((seq, HEAD_W)),
            _resident((seq, HEAD_W)),
        ]
        args += [ck, cv, cos, sin_signed, cos, sin_signed]
    in_specs += [_resident((4, ATTN_HEAD_DIM)), _resident((1, HEAD_W))]
    args += [lam_params, subln_g.reshape(1, HEAD_W)]
    return pl.pallas_call(
        functools.partial(_attn_kernel, has_ctx=has_ctx, lam_init=lam_init, seq=seq, past=past, heads=heads),
        grid=(stream.batch, ATTN_HEADS // heads, nq),
        in_specs=in_specs,
        out_specs=pl.BlockSpec((tq, width), lambda b, h, i: (b * nq + i, h)),
        out_shape=jax.ShapeDtypeStruct((stream.rows, D_MODEL), BF16),
        scratch_shapes=[pltpu.VMEM((heads, past + seq, HEAD_W), BF16),
                        pltpu.VMEM((heads, HEAD_W, past + seq), BF16)],
        compiler_params=_params(("parallel", "parallel", "arbitrary")),
        name="diff_attention",
    )(*args)


def _rope_tables(n_tokens):
    rows = n_tokens // GRID_W
    row = jnp.repeat(jnp.arange(rows), GRID_W).astype(F32)
    col = jnp.tile(jnp.arange(GRID_W), rows).astype(F32)
    n_freq = ATTN_HEAD_DIM // 4
    inv_freq = ROPE_BASE ** (-jnp.arange(n_freq, dtype=F32) / n_freq)
    ang_r = row[:, None] * inv_freq
    ang_c = col[:, None] * inv_freq
    ang = jnp.concatenate([ang_r, ang_r, ang_c, ang_c], axis=-1)
    sign = jnp.concatenate([-jnp.ones((n_freq,), F32), jnp.ones((n_freq,), F32)] * 2)
    cos = jnp.cos(ang)
    sin_signed = jnp.sin(ang) * sign
    return jnp.tile(cos, (1, 2)), jnp.tile(sin_signed, (1, 2))


def _boundary_rows(b, m, fwd):
    c = b.shape[0]
    idx = m - 1 if fwd else m
    if 2 * m >= SUBLANES:
        b3 = b.reshape(c // (2 * m), 2 * m, HGRN_DK)
        return jnp.broadcast_to(b3[:, idx:idx + 1, :], b3.shape).reshape(c, HGRN_DK)
    b3 = b.reshape(c // SUBLANES, SUBLANES, HGRN_DK)
    trow = lax.broadcasted_iota(jnp.int32, b3.shape, 1)
    br = jnp.broadcast_to(b3[:, idx:idx + 1, :], b3.shape)
    for start in range(2 * m, SUBLANES, 2 * m):
        br = jnp.where(trow >= start, b3[:, start + idx:start + idx + 1, :], br)
    return br.reshape(c, HGRN_DK)


def _gla_chunk(q, k, v, logf, st, tri, level, sides, fwd):
    c = q.shape[0]
    hi, mid, lo = _split3(logf)
    yield
    b = _bdot(tri, hi) + _bdot(tri, mid) + _bdot(tri, lo)
    yield
    b_last = b[c - 1:c, :] if fwd else b[0:1, :]
    qe = (q * jnp.exp(b)).astype(BF16)
    ke = (k * jnp.exp(b_last - b)).astype(BF16)
    yield
    o = _bdot_nt(qe, st.astype(BF16))
    st_new = st * jnp.exp(b_last) + _bdot(v.T.astype(BF16), ke)
    yield

    sc = jnp.where(level == -1, jnp.sum(q * k, axis=-1, keepdims=True), 0.0)
    pending = None
    for lg, side in enumerate(sides):
        ex = jnp.exp((b - _boundary_rows(b, 2 ** lg, fwd)) * side)
        z = (jnp.where(side > 0.0, q, k) * ex).astype(BF16)
        yield
        if pending is not None:
            sc = jnp.where(level == pending[0], pending[1], sc)
        pending = (lg, _bdot_nt(z, z))
    yield
    sc = jnp.where(level == pending[0], pending[1], sc)
    return o + _bdot(sc.astype(BF16), v.astype(BF16)), st_new


def _hgrn_kernel(*refs, has_state, emit_state, seq):
    q_ref, lff_ref, kf_ref, lfb_ref, kb_ref, i_ref, g_ref, ng_ref = refs[:8]
    rest = list(refs[8:])
    s0_refs = [rest.pop(0), rest.pop(0)] if has_state else [None, None]
    o_ref = rest.pop(0)
    sfin_ref = rest.pop(0) if emit_state else None
    of_scr, ob_scr, stf_scr, stb_scr = rest
    c = GLA_CHUNK
    n_chunks = seq // c

    ti = lax.broadcasted_iota(jnp.int32, (c, c), 0)
    si = lax.broadcasted_iota(jnp.int32, (c, c), 1)
    parted = 31 - lax.clz(ti ^ si)
    row = lax.broadcasted_iota(jnp.int32, (c, HGRN_DK), 0)
    later = [jnp.where((row & (1 << lg)) != 0, 1.0, -1.0) for lg in range(int(math.log2(c)))]
    dirs = (
        (True, lff_ref, kf_ref, stf_scr, of_scr, jnp.where(si <= ti, 1.0, 0.0).astype(BF16),
         jnp.where(si <= ti, parted, -2), later),
        (False, lfb_ref, kb_ref, stb_scr, ob_scr, jnp.where(si >= ti, 1.0, 0.0).astype(BF16),
         jnp.where(si >= ti, parted, -2), [-x for x in later]),
    )
    for h in range(GLA_HEADS):
        for d in range(2):
            dirs[d][3][h] = s0_refs[d][h].T if has_state else jnp.zeros((HGRN_DV, HGRN_DK), F32)

    def body(i, carry):
        chains, sinks = [], []
        for h in range(GLA_HEADS):
            cols = slice(h * HGRN_DK, (h + 1) * HGRN_DK)
            for fwd, lf_ref, k_ref, st_scr, o_scr, tri, level, sides in dirs:
                ci = i if fwd else n_chunks - 1 - i
                rows = pl.ds(pl.multiple_of(ci * c, c), c)
                chains.append(_gla_chunk(q_ref[rows, cols], k_ref[rows, cols], i_ref[rows, cols],
                                         lf_ref[rows, cols], st_scr[h], tri, level, sides, fwd))
                sinks.append((st_scr, o_scr, h, rows, cols))
        for (o, st_new), (st_scr, o_scr, h, rows, cols) in zip(_interleave(chains), sinks):
            st_scr[h] = st_new
            o_scr[rows, cols] = o
        return carry

    lax.fori_loop(0, n_chunks, body, 0)
    for h in range(GLA_HEADS):
        cols = slice(h * HGRN_DK, (h + 1) * HGRN_DK)
        if emit_state:
            sfin_ref[0, h] = stf_scr[h].T
            sfin_ref[1, h] = stb_scr[h].T
        o_ref[:, cols] = (_rms(of_scr[:, cols] + ob_scr[:, cols], ng_ref[...])
                          * _silu(g_ref[:, cols])).astype(BF16)


def _hgrn(stream, parts, norm_g, state=None, state_layer=0, emit_state=False):
    seq = stream.seq
    width = GLA_HEADS * HGRN_DK
    has_state = state is not None
    col = pl.BlockSpec((seq, width), lambda b, h: (b, h))
    in_specs = [col] * 7 + [_resident((1, HGRN_DV))]
    args = list(parts) + [norm_g.reshape(1, HGRN_DV)]
    if has_state:
        for d in range(2):
            in_specs.append(pl.BlockSpec((None, None, None, GLA_HEADS, HGRN_DK, HGRN_DV),
                                         lambda b, h, d=d: (b, state_layer, d, h, 0, 0)))
            args.append(state)
    out_specs = [pl.BlockSpec((seq, width), lambda b, h: (b, h))]
    out_shape = [jax.ShapeDtypeStruct((stream.rows, D_MODEL), BF16)]
    if emit_state:
        out_specs.append(pl.BlockSpec((None, 2, GLA_HEADS, HGRN_DK, HGRN_DV), lambda b, h: (b, 0, h, 0, 0)))
        out_shape.append(jax.ShapeDtypeStruct((stream.batch, 2, HGRN_HEADS, HGRN_DK, HGRN_DV), F32))
    res = pl.pallas_call(
        functools.partial(_hgrn_kernel, has_state=has_state, emit_state=emit_state, seq=seq),
        grid=(stream.batch, HGRN_HEADS // GLA_HEADS),
        in_specs=in_specs,
        out_specs=out_specs,
        out_shape=out_shape,
        scratch_shapes=[pltpu.VMEM((seq, width), F32), pltpu.VMEM((seq, width), F32),
                        pltpu.VMEM((GLA_HEADS, HGRN_DV, HGRN_DK), F32),
                        pltpu.VMEM((GLA_HEADS, HGRN_DV, HGRN_DK), F32)],
        compiler_params=_params(("parallel", "parallel")),
        name="hgrn2",
    )(*args)
    return res if emit_state else (res[0], None)


def _fill_padded(pad_scr, src_ref, seq):
    zeros = jnp.zeros((CONV_HALO, D_MODEL), F32)
    pad_scr[0:CONV_HALO, :] = zeros
    pad_scr[CONV_HALO:CONV_HALO + seq, :] = src_ref[...]
    pad_scr[CONV_HALO + seq:2 * CONV_HALO + seq, :] = zeros


def _dwconv_tile(pad_scr, w_ref, width):
    tile = CONV_TILE
    base = pl.program_id(1) * tile
    first = CONV_HALO - width // 2
    acc = None
    for r in range(SUBLANES):
        part = None
        for j in range(width):
            if (first + j) % SUBLANES != r:
                continue
            start = pl.multiple_of(base + (first + j - r), SUBLANES)
            term = pad_scr[pl.ds(start, tile + SUBLANES), :] * w_ref[j:j + 1, :]
            part = term if part is None else part + term
        if part is not None:
            shifted = part[r:r + tile, :]
            acc = shifted if acc is None else acc + shifted
    return acc


def _conformer_conv_kernel(u_ref, w_ref, bdw_ref, lng_ref, lnb_ref, o_ref, pad_scr, *, seq):
    @pl.when(pl.program_id(1) == 0)
    def _():
        _fill_padded(pad_scr, u_ref, seq)

    u = _dwconv_tile(pad_scr, w_ref, CONV_WIDTH) + bdw_ref[...]
    xc = u - jnp.mean(u, axis=-1, keepdims=True)
    y = xc * lax.rsqrt(jnp.mean(xc * xc, axis=-1, keepdims=True) + EPS) * lng_ref[...] + lnb_ref[...]
    o_ref[...] = _silu(y).astype(BF16)


def _short_conv_kernel(p_ref, bg_ref, w_ref, o_ref, pad_scr, *, seq):
    @pl.when(pl.program_id(1) == 0)
    def _():
        _fill_padded(pad_scr, p_ref, seq)

    o_ref[...] = (bg_ref[...] * _dwconv_tile(pad_scr, w_ref, SHORT_CONV_WIDTH)).astype(BF16)


def _conv_call(stream, body, seq_inputs, tile_inputs, small_inputs):
    seq, tile = stream.seq, CONV_TILE
    nt = seq // tile
    in_specs = ([pl.BlockSpec((seq, D_MODEL), lambda b, t: (b, 0))] * len(seq_inputs)
                + [pl.BlockSpec((tile, D_MODEL), lambda b, t: (b * nt + t, 0))] * len(tile_inputs)
                + [_resident(a.shape) for a in small_inputs])
    return pl.pallas_call(
        functools.partial(body, seq=seq),
        grid=(stream.batch, nt),
        in_specs=in_specs,
        out_specs=pl.BlockSpec((tile, D_MODEL), lambda b, t: (b * nt + t, 0)),
        out_shape=jax.ShapeDtypeStruct((stream.rows, D_MODEL), BF16),
        scratch_shapes=[pltpu.VMEM((seq + 2 * CONV_HALO, D_MODEL), F32)],
        compiler_params=_params(("parallel", "arbitrary")),
        name="dwconv",
    )(*seq_inputs, *tile_inputs, *small_inputs)


def kernel(x_prompt, x_sample, c, cache_k, cache_v, state_hgrn, c_ctx, w_mod, b_mod, norm_g,
           w_ffn_in, w_ffn_out, w_attn_qkv, w_attn_o, attn_lambda, attn_subln_g,
           w_hgrn_in, w_hgrn_o, hgrn_norm_g, hgrn_lb, w_cm_in, b_cm_in, w_cm_dw, b_cm_dw,
           cm_ln_g, cm_ln_b, w_cm_out, b_cm_out, w_sc_in, w_sc_conv, w_sc_out):
    batch, seq, _ = x_prompt.shape
    dec_batch, dec_seq, _ = x_sample.shape
    past = cache_k.shape[2]
    prompt = _Stream(batch, seq, 0, per_batch_mod=False)
    sample = _Stream(dec_batch, dec_seq, 1, per_batch_mod=True)
    streams = (prompt, sample)

    cvec = jnp.concatenate([c_ctx[None], c, jnp.zeros((SUBLANES - 1 - dec_batch, D_MODEL), F32)], axis=0)
    mods = _modulation(cvec, w_mod, b_mod).reshape(DEPTH, SUBLANES, 3, 3, D_MODEL)

    rope = _rope_tables(dec_seq)

    ffn_w = [w_ffn_in[0, 0].astype(BF16), w_ffn_out[0, 0].astype(BF16)]
    ffn_src = (w_ffn_in, w_ffn_out)

    def ffn_sublayer(n, mixers):
        i, s = divmod(n, 2)
        m, g = mods[i, :, 2 * s], norm_g[i, 2 * s]
        nxt = divmod(n + 1, 2) if n + 1 < 2 * DEPTH else None
        w_in, w_out = ffn_w
        for si, st in enumerate(streams):
            cast = (ffn_src[si],) + nxt if nxt is not None else None
            ys[si], converted = _ffn(st, ys[si], m, g, w_in, w_out, 0.5,
                                     mixer=None if mixers is None else mixers[si], cast=cast)
            if nxt is not None:
                ffn_w[si] = converted

    ys = [x_prompt.reshape(prompt.rows, D_MODEL), x_sample.reshape(sample.rows, D_MODEL)]
    new_k = new_v = new_s = None
    for i in range(DEPTH):
        kind, j = i % N_MIXERS, i // N_MIXERS
        g = norm_g[i]
        ffn_sublayer(2 * i, None)

        m1 = mods[i, :, 1]
        mixers = []
        for si, st in enumerate(streams):
            y = ys[si]
            is_prompt = si == 0
            bias_out = None
            if kind == 0:
                q, k, v = _proj(st, y, m1, g[1], w_attn_qkv[j].astype(BF16), None, "split", 3)
                if is_prompt:
                    new_k = k.reshape(batch, 1, seq, ATTN_HEADS, HEAD_W)
                    new_v = v.reshape(batch, 1, seq, ATTN_HEADS, HEAD_W)
                    mix = _attention(st, q, k, v, attn_lambda[j], attn_subln_g[j], i, ATTN_HEADS)
                else:
                    ctx = (cache_k.reshape(dec_batch, -1, D_MODEL), cache_v.reshape(dec_batch, -1, D_MODEL),
                           j, past, rope[0], rope[1])
                    mix = _attention(st, q, k, v, attn_lambda[j], attn_subln_g[j], i, ATTN_HEAD_GROUP, ctx)
                w_o = w_attn_o[j]
            elif kind == 1:
                parts = _proj(st, y, m1, g[1], w_hgrn_in[j].astype(BF16), None, "hgrn", 7,
                              lb=hgrn_lb, layer_idx=i)
                if is_prompt:
                    mix, s_fin = _hgrn(st, parts, hgrn_norm_g[j], emit_state=True)
                    new_s = s_fin[:, None]
                else:
                    mix, _ = _hgrn(st, parts, hgrn_norm_g[j], state=state_hgrn, state_layer=j)
                w_o = w_hgrn_o[j]
            elif kind == 2:
                (u,) = _proj(st, y, m1, g[1], w_cm_in[j].astype(BF16), b_cm_in[j], "glu", 1)
                mix = _conv_call(st, _conformer_conv_kernel, [u], [],
                                 [w_cm_dw[j], b_cm_dw[j].reshape(1, D_MODEL),
                                  cm_ln_g[j].reshape(1, D_MODEL), cm_ln_b[j].reshape(1, D_MODEL)])
                w_o = w_cm_out[j]
                bias_out = b_cm_out[j]
            else:
                bg, prod = _proj(st, y, m1, g[1], w_sc_in[j].astype(BF16), None, "gated_pair", 2)
                mix = _conv_call(st, _short_conv_kernel, [prod], [bg], [w_sc_conv[j]])
                w_o = w_sc_out[j]
            mixers.append((mix, w_o.astype(BF16), bias_out, m1, g[1]))

        ffn_sublayer(2 * i + 1, mixers)

    return (ys[0].reshape(batch, seq, D_MODEL), ys[1].reshape(dec_batch, dec_seq, D_MODEL),
            new_k, new_v, new_s)
```

```python
import functools
import math

import jax
import jax.numpy as jnp
from jax import lax
from jax.experimental import pallas as pl
from jax.experimental.pallas import tpu as pltpu

D_MODEL = 1024
DEPTH = 4
GRID_W = 64
N_MIXERS = 4
ATTN_HEAD_DIM = 64
ATTN_HEADS = D_MODEL // (2 * ATTN_HEAD_DIM)
HEAD_W = 2 * ATTN_HEAD_DIM
ROPE_BASE = 10000.0
HGRN_DK = 128
HGRN_HEADS = D_MODEL // HGRN_DK
HGRN_DV = D_MODEL // HGRN_HEADS
CONV_WIDTH = 31
SHORT_CONV_WIDTH = 3
D_FF = ((8 * D_MODEL // 3 + 127) // 128) * 128
EPS = 1e-6

F32 = jnp.float32
BF16 = jnp.bfloat16

SUBLANES = 8
FFN_TILE = 1024
TOKEN_TILE = 512
SUB_ROWS = 256
ATTN_Q_TILE = 512
ATTN_KEY_BLOCK = 512
ATTN_HEAD_GROUP = 2
GLA_CHUNK = 128
CONV_TILE = 256
CONV_HALO = 16
MOD_TILE = 2304
VMEM_LIMIT = 56 * 1024 * 1024
FFN_VMEM_LIMIT = 60 * 1024 * 1024
LOGF_MIN = -180.0
GLA_HEADS = 2


def _bdot(a, b):
    return jnp.dot(a, b, preferred_element_type=F32)


def _bdot_nt(a, b):
    return lax.dot_general(a, b, (((1,), (1,)), ((), ())), preferred_element_type=F32)


def _rms(x, g):
    return x * lax.rsqrt(jnp.mean(x * x, axis=-1, keepdims=True) + EPS) * g


def _silu(x):
    return x * jax.nn.sigmoid(x)


def _split2(x):
    hi = x.astype(BF16)
    lo = (x - hi.astype(F32)).astype(BF16)
    return hi, lo


def _split3(x):
    hi = x.astype(BF16)
    r = x - hi.astype(F32)
    mid = r.astype(BF16)
    lo = (r - mid.astype(F32)).astype(BF16)
    return hi, mid, lo


def _params(sem, vmem=VMEM_LIMIT):
    return pltpu.CompilerParams(dimension_semantics=sem, vmem_limit_bytes=vmem)


def _interleave(gens):
    results = [None] * len(gens)
    active = list(enumerate(gens))
    while active:
        still = []
        for i, g in active:
            try:
                next(g)
                still.append((i, g))
            except StopIteration as done:
                results[i] = done.value
        active = still
    return results


def _resident(shape):
    return pl.BlockSpec(shape, lambda *_: (0,) * len(shape), pipeline_mode=pl.Buffered(1))


def _mod_kernel(c_ref, w_ref, b_ref, o_ref):
    a_hi, a_lo = _split2(_silu(c_ref[...]))
    w_hi, w_lo = _split2(w_ref[0])
    acc = _bdot(a_hi, w_hi) + _bdot(a_lo, w_hi) + _bdot(a_hi, w_lo)
    o_ref[0] = acc + b_ref[0]


def _modulation(cvec, w_mod, b_mod):
    rows = cvec.shape[0]
    n = w_mod.shape[-1]
    return pl.pallas_call(
        _mod_kernel,
        grid=(DEPTH, n // MOD_TILE),
        in_specs=[
            pl.BlockSpec((rows, D_MODEL), lambda i, j: (0, 0)),
            pl.BlockSpec((1, D_MODEL, MOD_TILE), lambda i, j: (i, 0, j)),
            pl.BlockSpec((1, 1, MOD_TILE), lambda i, j: (i, 0, j)),
        ],
        out_specs=pl.BlockSpec((1, rows, MOD_TILE), lambda i, j: (i, 0, j)),
        out_shape=jax.ShapeDtypeStruct((DEPTH, rows, n), F32),
        compiler_params=_params(("arbitrary", "arbitrary")),
        name="modulation",
    )(cvec, w_mod, b_mod.reshape(DEPTH, 1, n))


class _Stream:
    def __init__(self, batch, seq, mod_row0, per_batch_mod):
        self.batch = batch
        self.seq = seq
        self.rows = batch * seq
        self.mod_row0 = mod_row0
        self.tokens_per_mod = seq if per_batch_mod else batch * seq

    def mod_spec(self, tile):
        row0, tpm = self.mod_row0, self.tokens_per_mod
        assert tpm % tile == 0
        return pl.BlockSpec((1, 3, D_MODEL), lambda t, *_: (row0 + (t * tile) // tpm, 0, 0))


def _row_groups(ref):
    return [pl.ds(s, SUB_ROWS) for s in range(0, ref.shape[0], SUB_ROWS)]


def _ffn_kernel(*refs, weight, has_mixer, mixer_bias, has_cast):
    it = iter(refs)
    x_ref = next(it)
    if has_mixer:
        mix_ref, wmix_ref = next(it), next(it)
        bmix_ref = next(it) if mixer_bias else None
        mm_ref, gm_ref = next(it), next(it)
    m_ref, g_ref, wi_ref, wo_ref = next(it), next(it), next(it), next(it)
    cast_src = next(it) if has_cast else None
    o_ref = next(it)
    if has_cast:
        next(it)[...] = cast_src[...].astype(BF16)

    m = m_ref[0]
    shift, scale, gate_m = m[0:1, :], 1.0 + m[1:2, :], weight * m[2:3, :]
    def residual_in(rows):
        x = x_ref[rows, :]
        if has_mixer:
            mixed = _bdot(mix_ref[rows, :], wmix_ref[...])
            if mixer_bias:
                mixed = mixed + bmix_ref[...]
            x = x + mm_ref[0][2:3, :] * _rms(mixed, gm_ref[1:2, :])
        return x

    groups = _row_groups(x_ref)
    x_next = residual_in(groups[0])
    for n, rows in enumerate(groups):
        x = x_next
        if n + 1 < len(groups):
            x_next = residual_in(groups[n + 1])
        h = (_rms(x, g_ref[0:1, :]) * scale + shift).astype(BF16)
        gate = _bdot(h, wi_ref[:, :D_FF])
        up = _bdot(h, wi_ref[:, D_FF:])
        act = (_silu(gate) * up).astype(BF16)
        out = _bdot(act, wo_ref[...])
        o_ref[rows, :] = x + gate_m * _rms(out, g_ref[1:2, :])


def _ffn(stream, y, m, g, w_in, w_out, weight, mixer=None, cast=None):
    tm = FFN_TILE
    steps = stream.rows // tm
    in_specs = [pl.BlockSpec((tm, D_MODEL), lambda t: (t, 0))]
    args = [y]
    mixer_bias = False
    if mixer is not None:
        mix, w_mix, b_mix, m_mix, g_mix = mixer
        mixer_bias = b_mix is not None
        in_specs += [pl.BlockSpec((tm, D_MODEL), lambda t: (t, 0)), _resident((D_MODEL, D_MODEL))]
        args += [mix, w_mix]
        if mixer_bias:
            in_specs.append(_resident((1, D_MODEL)))
            args.append(b_mix.reshape(1, D_MODEL))
        in_specs += [stream.mod_spec(tm), _resident((2, D_MODEL))]
        args += [m_mix, g_mix]
    in_specs += [stream.mod_spec(tm), _resident((2, D_MODEL)),
                 _resident((D_MODEL, 2 * D_FF)), _resident((D_FF, D_MODEL))]
    args += [m, g, w_in, w_out]
    out_specs = [pl.BlockSpec((tm, D_MODEL), lambda t: (t, 0))]
    out_shape = [jax.ShapeDtypeStruct((stream.rows, D_MODEL), F32)]
    if cast is not None:
        src, layer, which = cast
        r, c = src.shape[2:]
        in_specs.append(pl.BlockSpec((None, None, r // steps, c), lambda t: (layer, which, t, 0)))
        args.append(src)
        out_specs.append(pl.BlockSpec((r // steps, c), lambda t: (t, 0)))
        out_shape.append(jax.ShapeDtypeStruct((r, c), BF16))
    res = pl.pallas_call(
        functools.partial(_ffn_kernel, weight=weight, has_mixer=mixer is not None,
                          mixer_bias=mixer_bias, has_cast=cast is not None),
        grid=(steps,),
        in_specs=in_specs,
        out_specs=out_specs,
        out_shape=out_shape,
        compiler_params=_params(("parallel",), FFN_VMEM_LIMIT),
        name="ffn",
    )(*args)
    return (res[0], res[1]) if cast is not None else (res[0], None)


def _gla_gates(z, lbd):
    e = jnp.exp(-jnp.abs(z))
    r = 1.0 / (1.0 + e)
    pos = z >= 0.0
    sig = jnp.where(pos, r, e * r)
    sig_neg = jnp.where(pos, e * r, r)
    logf = jnp.maximum(jnp.log(lbd + (1.0 - lbd) * sig), LOGF_MIN)
    return logf, (1.0 - lbd) * sig_neg


def _proj_kernel(*refs, mode, has_bias, layer_idx):
    x_ref, m_ref, g_ref, w_ref = refs[:4]
    refs = refs[4:]
    b_ref = lb_ref = None
    if has_bias:
        b_ref, refs = refs[0], refs[1:]
    if mode == "hgrn":
        lb_ref, refs = refs[0], refs[1:]
    outs = refs
    m = m_ref[0]
    shift, scale = m[0:1, :], 1.0 + m[1:2, :]

    lower = []
    if mode == "hgrn":
        for d in range(2):
            lbp = lb_ref[d]
            e = jnp.exp(lbp - jnp.max(lbp, axis=0, keepdims=True))
            p = e / jnp.sum(e, axis=0, keepdims=True)
            lower.append(jnp.sum(p[1:layer_idx + 1, :], axis=0, keepdims=True))

    for rows in _row_groups(x_ref):
        h = (_rms(x_ref[rows, :], g_ref[0:1, :]) * scale + shift).astype(BF16)

        def part(p):
            r = _bdot(h, w_ref[:, p * D_MODEL:(p + 1) * D_MODEL])
            if has_bias:
                r = r + b_ref[:, p * D_MODEL:(p + 1) * D_MODEL]
            return r

        if mode == "split":
            for p, o_ref in enumerate(outs):
                o_ref[rows, :] = part(p)
        elif mode == "glu":
            a = part(0)
            outs[0][rows, :] = a * jax.nn.sigmoid(part(1))
        elif mode == "gated_pair":
            outs[0][rows, :] = part(0)
            outs[1][rows, :] = part(1) * part(2)
        else:
            outs[0][rows, :] = part(0)
            for d in range(2):
                logf, k = _gla_gates(part(1 + d), lower[d])
                outs[1 + 2 * d][rows, :] = logf
                outs[2 + 2 * d][rows, :] = k
            outs[5][rows, :] = part(3)
            outs[6][rows, :] = part(4)


def _proj(stream, y, m, g, w, b, mode, n_out, lb=None, layer_idx=0):
    tm = TOKEN_TILE
    n = w.shape[1]
    has_bias = b is not None
    in_specs = [
        pl.BlockSpec((tm, D_MODEL), lambda t: (t, 0)),
        stream.mod_spec(tm),
        _resident((2, D_MODEL)),
        _resident((D_MODEL, n)),
    ]
    args = [y, m, g, w]
    if has_bias:
        in_specs.append(_resident((1, n)))
        args.append(b.reshape(1, n))
    if mode == "hgrn":
        in_specs.append(_resident(lb.shape))
        args.append(lb)
    return pl.pallas_call(
        functools.partial(_proj_kernel, mode=mode, has_bias=has_bias, layer_idx=layer_idx),
        grid=(stream.rows // tm,),
        in_specs=in_specs,
        out_specs=[pl.BlockSpec((tm, D_MODEL), lambda t: (t, 0))] * n_out,
        out_shape=[jax.ShapeDtypeStruct((stream.rows, D_MODEL), F32)] * n_out,
        compiler_params=_params(("parallel",)),
        name="proj_" + mode,
    )(*args)


def _reduce_rows(x, op):
    r = x.shape[0]
    while r > SUBLANES:
        fold = next((f for f in (4, 2) if r % (f * SUBLANES) == 0), None)
        if fold is None:
            break
        x = op(x.reshape(fold, r // fold, x.shape[1]), axis=0)
        r //= fold
    return op(x, axis=0, keepdims=True)


def _rope(x, cos, sin_signed):
    lane = lax.broadcasted_iota(jnp.int32, x.shape, 1)
    first = (lane % (ATTN_HEAD_DIM // 2)) < (ATTN_HEAD_DIM // 4)
    rot = jnp.where(first, pltpu.roll(x, HEAD_W - ATTN_HEAD_DIM // 4, 1),
                    pltpu.roll(x, ATTN_HEAD_DIM // 4, 1))
    return x * cos + rot * sin_signed


def _attn_kernel(*refs, has_ctx, lam_init, seq, past, heads):
    if has_ctx:
        (q_ref, k_ref, v_ref, ck_ref, cv_ref, cosq_ref, sinq_ref, cosk_ref, sink_ref,
         lam_ref, g_ref, o_ref, k_scr, vt_scr) = refs
    else:
        q_ref, k_ref, v_ref, lam_ref, g_ref, o_ref, k_scr, vt_scr = refs

    def head_cols(h):
        return slice(h * HEAD_W, (h + 1) * HEAD_W)

    @pl.when(pl.program_id(2) == 0)
    def _():
        for h in range(heads):
            cols = head_cols(h)
            if has_ctx:
                k_scr[h, 0:past, :] = ck_ref[0, :, cols].astype(BF16)
                vt_scr[h, :, 0:past] = cv_ref[0, :, cols].T.astype(BF16)
                k_scr[h, past:past + seq, :] = _rope(k_ref[:, cols], cosk_ref[...], sink_ref[...]).astype(BF16)
                vt_scr[h, :, past:past + seq] = v_ref[:, cols].T.astype(BF16)
            else:
                k_scr[h] = k_ref[:, cols].astype(BF16)
                vt_scr[h] = v_ref[:, cols].T.astype(BF16)

    lp = lam_ref[...]
    lam = (jnp.exp(jnp.sum(lp[0:1, :] * lp[1:2, :], axis=-1, keepdims=True))
           - jnp.exp(jnp.sum(lp[2:3, :] * lp[3:4, :], axis=-1, keepdims=True)) + lam_init)

    n_keys = past + seq
    key_block = min(ATTN_KEY_BLOCK, n_keys)

    def softmax_pv(h, qm):
        def scores(j):
            return _bdot_nt(k_scr[h, j:j + key_block, :], qm)

        m = l = acc = None
        s_next = scores(0)
        for j in range(0, n_keys, key_block):
            s = s_next
            if j + key_block < n_keys:
                s_next = scores(j + key_block)
            yield
            mj = _reduce_rows(s, jnp.max)
            m_new = mj if m is None else jnp.maximum(m, mj)
            p = jnp.exp(s - m_new)
            lj = _reduce_rows(p, jnp.sum)
            yield
            pv = _bdot(vt_scr[h, :, j:j + key_block], p.astype(BF16))
            if m is None:
                l, acc = lj, pv
            else:
                alpha = jnp.exp(m - m_new)
                l, acc = alpha * l + lj, alpha * acc + pv
            m = m_new
            yield
        return acc / l

    for h0 in range(0, heads, ATTN_HEAD_GROUP):
        group = range(h0, min(h0 + ATTN_HEAD_GROUP, heads))
        branches = []
        for h in group:
            q = q_ref[:, head_cols(h)]
            if has_ctx:
                q = _rope(q, cosq_ref[...], sinq_ref[...])
            q = q * (ATTN_HEAD_DIM ** -0.5)
            lane = lax.broadcasted_iota(jnp.int32, q.shape, 1)
            branches.append(softmax_pv(h, jnp.where(lane < ATTN_HEAD_DIM, q, 0.0).astype(BF16)))
            branches.append(softmax_pv(h, jnp.where(lane >= ATTN_HEAD_DIM, q, 0.0).astype(BF16)))
        outs = _interleave(branches)
        for n, h in enumerate(group):
            ot = outs[2 * n] - lam * outs[2 * n + 1]
            o_ref[:, head_cols(h)] = (_rms(ot.T, g_ref[...]) * (1.0 - lam_init)).astype(BF16)


def _attention(stream, q, k, v, lam_params, subln_g, layer_idx, heads, ctx=None):
    seq, tq = stream.seq, min(ATTN_Q_TILE, stream.seq)
    nq = seq // tq
    width = heads * HEAD_W
    lam_init = 0.8 - 0.6 * math.exp(-0.3 * layer_idx)
    has_ctx = ctx is not None
    in_specs = [
        pl.BlockSpec((tq, width), lambda b, h, i: (b * nq + i, h)),
        pl.BlockSpec((seq, width), lambda b, h, i: (b, h)),
        pl.BlockSpec((seq, width), lambda b, h, i: (b, h)),
    ]
    args = [q, k, v]
    past = 0
    if has_ctx:
        ck, cv, cache_layer, past, cos, sin_signed = ctx
        in_specs += [
            pl.BlockSpec((1, past, width), lambda b, h, i: (b, cache_layer, h)),
            pl.BlockSpec((1, past, width), lambda b, h, i: (b, cache_layer, h)),
            pl.BlockSpec((tq, HEAD_W), lambda b, h, i: (i, 0)),
            pl.BlockSpec((tq, HEAD_W), lambda b, h, i: (i, 0)),
            _resident((seq, HEAD_W)),
            _resident((seq, HEAD_W)),
        ]
        args += [ck, cv, cos, sin_signed, cos, sin_signed]
    in_specs += [_resident((4, ATTN_HEAD_DIM)), _resident((1, HEAD_W))]
    args += [lam_params, subln_g.reshape(1, HEAD_W)]
    return pl.pallas_call(
        functools.partial(_attn_kernel, has_ctx=has_ctx, lam_init=lam_init, seq=seq, past=past, heads=heads),
        grid=(stream.batch, ATTN_HEADS // heads, nq),
        in_specs=in_specs,
        out_specs=pl.BlockSpec((tq, width), lambda b, h, i: (b * nq + i, h)),
        out_shape=jax.ShapeDtypeStruct((stream.rows, D_MODEL), BF16),
        scratch_shapes=[pltpu.VMEM((heads, past + seq, HEAD_W), BF16),
                        pltpu.VMEM((heads, HEAD_W, past + seq), BF16)],
        compiler_params=_params(("parallel", "parallel", "arbitrary")),
        name="diff_attention",
    )(*args)


def _rope_tables(n_tokens):
    rows = n_tokens // GRID_W
    row = jnp.repeat(jnp.arange(rows), GRID_W).astype(F32)
    col = jnp.tile(jnp.arange(GRID_W), rows).astype(F32)
    n_freq = ATTN_HEAD_DIM // 4
    inv_freq = ROPE_BASE ** (-jnp.arange(n_freq, dtype=F32) / n_freq)
    ang_r = row[:, None] * inv_freq
    ang_c = col[:, None] * inv_freq
    ang = jnp.concatenate([ang_r, ang_r, ang_c, ang_c], axis=-1)
    sign = jnp.concatenate([-jnp.ones((n_freq,), F32), jnp.ones((n_freq,), F32)] * 2)
    cos = jnp.cos(ang)
    sin_signed = jnp.sin(ang) * sign
    return jnp.tile(cos, (1, 2)), jnp.tile(sin_signed, (1, 2))


def _boundary_rows(b, m, fwd):
    c = b.shape[0]
    idx = m - 1 if fwd else m
    if 2 * m >= SUBLANES:
        b3 = b.reshape(c // (2 * m), 2 * m, HGRN_DK)
        return jnp.broadcast_to(b3[:, idx:idx + 1, :], b3.shape).reshape(c, HGRN_DK)
    b3 = b.reshape(c // SUBLANES, SUBLANES, HGRN_DK)
    trow = lax.broadcasted_iota(jnp.int32, b3.shape, 1)
    br = jnp.broadcast_to(b3[:, idx:idx + 1, :], b3.shape)
    for start in range(2 * m, SUBLANES, 2 * m):
        br = jnp.where(trow >= start, b3[:, start + idx:start + idx + 1, :], br)
    return br.reshape(c, HGRN_DK)


def _gla_chunk(q, k, v, logf, st, tri, level, sides, fwd):
    c = q.shape[0]
    hi, mid, lo = _split3(logf)
    yield
    b = _bdot(tri, hi) + _bdot(tri, mid) + _bdot(tri, lo)
    yield
    b_last = b[c - 1:c, :] if fwd else b[0:1, :]
    qe = (q * jnp.exp(b)).astype(BF16)
    ke = (k * jnp.exp(b_last - b)).astype(BF16)
    yield
    o = _bdot_nt(qe, st.astype(BF16))
    st_new = st * jnp.exp(b_last) + _bdot(v.T.astype(BF16), ke)
    yield

    sc = jnp.where(level == -1, jnp.sum(q * k, axis=-1, keepdims=True), 0.0)
    pending = None
    for lg, side in enumerate(sides):
        ex = jnp.exp((b - _boundary_rows(b, 2 ** lg, fwd)) * side)
        z = (jnp.where(side > 0.0, q, k) * ex).astype(BF16)
        yield
        if pending is not None:
            sc = jnp.where(level == pending[0], pending[1], sc)
        pending = (lg, _bdot_nt(z, z))
    yield
    sc = jnp.where(level == pending[0], pending[1], sc)
    return o + _bdot(sc.astype(BF16), v.astype(BF16)), st_new


def _hgrn_kernel(*refs, has_state, emit_state, seq):
    q_ref, lff_ref, kf_ref, lfb_ref, kb_ref, i_ref, g_ref, ng_ref = refs[:8]
    rest = list(refs[8:])
    s0_refs = [rest.pop(0), rest.pop(0)] if has_state else [None, None]
    o_ref = rest.pop(0)
    sfin_ref = rest.pop(0) if emit_state else None
    of_scr, ob_scr, stf_scr, stb_scr = rest
    c = GLA_CHUNK
    n_chunks = seq // c

    ti = lax.broadcasted_iota(jnp.int32, (c, c), 0)
    si = lax.broadcasted_iota(jnp.int32, (c, c), 1)
    parted = 31 - lax.clz(ti ^ si)
    row = lax.broadcasted_iota(jnp.int32, (c, HGRN_DK), 0)
    later = [jnp.where((row & (1 << lg)) != 0, 1.0, -1.0) for lg in range(int(math.log2(c)))]
    dirs = (
        (True, lff_ref, kf_ref, stf_scr, of_scr, jnp.where(si <= ti, 1.0, 0.0).astype(BF16),
         jnp.where(si <= ti, parted, -2), later),
        (False, lfb_ref, kb_ref, stb_scr, ob_scr, jnp.where(si >= ti, 1.0, 0.0).astype(BF16),
         jnp.where(si >= ti, parted, -2), [-x for x in later]),
    )
    for h in range(GLA_HEADS):
        for d in range(2):
            dirs[d][3][h] = s0_refs[d][h].T if has_state else jnp.zeros((HGRN_DV, HGRN_DK), F32)

    def body(i, carry):
        chains, sinks = [], []
        for h in range(GLA_HEADS):
            cols = slice(h * HGRN_DK, (h + 1) * HGRN_DK)
            for fwd, lf_ref, k_ref, st_scr, o_scr, tri, level, sides in dirs:
                ci = i if fwd else n_chunks - 1 - i
                rows = pl.ds(pl.multiple_of(ci * c, c), c)
                chains.append(_gla_chunk(q_ref[rows, cols], k_ref[rows, cols], i_ref[rows, cols],
                                         lf_ref[rows, cols], st_scr[h], tri, level, sides, fwd))
                sinks.append((st_scr, o_scr, h, rows, cols))
        for (o, st_new), (st_scr, o_scr, h, rows, cols) in zip(_interleave(chains), sinks):
            st_scr[h] = st_new
            o_scr[rows, cols] = o
        return carry

    lax.fori_loop(0, n_chunks, body, 0)
    for h in range(GLA_HEADS):
        cols = slice(h * HGRN_DK, (h + 1) * HGRN_DK)
        if emit_state:
            sfin_ref[0, h] = stf_scr[h].T
            sfin_ref[1, h] = stb_scr[h].T
        o_ref[:, cols] = (_rms(of_scr[:, cols] + ob_scr[:, cols], ng_ref[...])
                          * _silu(g_ref[:, cols])).astype(BF16)


def _hgrn(stream, parts, norm_g, state=None, state_layer=0, emit_state=False):
    seq = stream.seq
    width = GLA_HEADS * HGRN_DK
    has_state = state is not None
    col = pl.BlockSpec((seq, width), lambda b, h: (b, h))
    in_specs = [col] * 7 + [_resident((1, HGRN_DV))]
    args = list(parts) + [norm_g.reshape(1, HGRN_DV)]
    if has_state:
        for d in range(2):
            in_specs.append(pl.BlockSpec((None, None, None, GLA_HEADS, HGRN_DK, HGRN_DV),
                                         lambda b, h, d=d: (b, state_layer, d, h, 0, 0)))
            args.append(state)
    out_specs = [pl.BlockSpec((seq, width), lambda b, h: (b, h))]
    out_shape = [jax.ShapeDtypeStruct((stream.rows, D_MODEL), BF16)]
    if emit_state:
        out_specs.append(pl.BlockSpec((None, 2, GLA_HEADS, HGRN_DK, HGRN_DV), lambda b, h: (b, 0, h, 0, 0)))
        out_shape.append(jax.ShapeDtypeStruct((stream.batch, 2, HGRN_HEADS, HGRN_DK, HGRN_DV), F32))
    res = pl.pallas_call(
        functools.partial(_hgrn_kernel, has_state=has_state, emit_state=emit_state, seq=seq),
        grid=(stream.batch, HGRN_HEADS // GLA_HEADS),
        in_specs=in_specs,
        out_specs=out_specs,
        out_shape=out_shape,
        scratch_shapes=[pltpu.VMEM((seq, width), F32), pltpu.VMEM((seq, width), F32),
                        pltpu.VMEM((GLA_HEADS, HGRN_DV, HGRN_DK), F32),
                        pltpu.VMEM((GLA_HEADS, HGRN_DV, HGRN_DK), F32)],
        compiler_params=_params(("parallel", "parallel")),
        name="hgrn2",
    )(*args)
    return res if emit_state else (res[0], None)


def _fill_padded(pad_scr, src_ref, seq):
    zeros = jnp.zeros((CONV_HALO, D_MODEL), F32)
    pad_scr[0:CONV_HALO, :] = zeros
    pad_scr[CONV_HALO:CONV_HALO + seq, :] = src_ref[...]
    pad_scr[CONV_HALO + seq:2 * CONV_HALO + seq, :] = zeros


def _dwconv_tile(pad_scr, w_ref, width):
    tile = CONV_TILE
    base = pl.program_id(1) * tile
    first = CONV_HALO - width // 2
    acc = None
    for r in range(SUBLANES):
        part = None
        for j in range(width):
            if (first + j) % SUBLANES != r:
                continue
            start = pl.multiple_of(base + (first + j - r), SUBLANES)
            term = pad_scr[pl.ds(start, tile + SUBLANES), :] * w_ref[j:j + 1, :]
            part = term if part is None else part + term
        if part is not None:
            shifted = part[r:r + tile, :]
            acc = shifted if acc is None else acc + shifted
    return acc


def _conformer_conv_kernel(u_ref, w_ref, bdw_ref, lng_ref, lnb_ref, o_ref, pad_scr, *, seq):
    @pl.when(pl.program_id(1) == 0)
    def _():
        _fill_padded(pad_scr, u_ref, seq)

    u = _dwconv_tile(pad_scr, w_ref, CONV_WIDTH) + bdw_ref[...]
    xc = u - jnp.mean(u, axis=-1, keepdims=True)
    y = xc * lax.rsqrt(jnp.mean(xc * xc, axis=-1, keepdims=True) + EPS) * lng_ref[...] + lnb_ref[...]
    o_ref[...] = _silu(y).astype(BF16)


def _short_conv_kernel(p_ref, bg_ref, w_ref, o_ref, pad_scr, *, seq):
    @pl.when(pl.program_id(1) == 0)
    def _():
        _fill_padded(pad_scr, p_ref, seq)

    o_ref[...] = (bg_ref[...] * _dwconv_tile(pad_scr, w_ref, SHORT_CONV_WIDTH)).astype(BF16)


def _conv_call(stream, body, seq_inputs, tile_inputs, small_inputs):
    seq, tile = stream.seq, CONV_TILE
    nt = seq // tile
    in_specs = ([pl.BlockSpec((seq, D_MODEL), lambda b, t: (b, 0))] * len(seq_inputs)
                + [pl.BlockSpec((tile, D_MODEL), lambda b, t: (b * nt + t, 0))] * len(tile_inputs)
                + [_resident(a.shape) for a in small_inputs])
    return pl.pallas_call(
        functools.partial(body, seq=seq),
        grid=(stream.batch, nt),
        in_specs=in_specs,
        out_specs=pl.BlockSpec((tile, D_MODEL), lambda b, t: (b * nt + t, 0)),
        out_shape=jax.ShapeDtypeStruct((stream.rows, D_MODEL), BF16),
        scratch_shapes=[pltpu.VMEM((seq + 2 * CONV_HALO, D_MODEL), F32)],
        compiler_params=_params(("parallel", "arbitrary")),
        name="dwconv",
    )(*seq_inputs, *tile_inputs, *small_inputs)


def kernel(x_prompt, x_sample, c, cache_k, cache_v, state_hgrn, c_ctx, w_mod, b_mod, norm_g,
           w_ffn_in, w_ffn_out, w_attn_qkv, w_attn_o, attn_lambda, attn_subln_g,
           w_hgrn_in, w_hgrn_o, hgrn_norm_g, hgrn_lb, w_cm_in, b_cm_in, w_cm_dw, b_cm_dw,
           cm_ln_g, cm_ln_b, w_cm_out, b_cm_out, w_sc_in, w_sc_conv, w_sc_out):
    batch, seq, _ = x_prompt.shape
    dec_batch, dec_seq, _ = x_sample.shape
    past = cache_k.shape[2]
    prompt = _Stream(batch, seq, 0, per_batch_mod=False)
    sample = _Stream(dec_batch, dec_seq, 1, per_batch_mod=True)
    streams = (prompt, sample)

    cvec = jnp.concatenate([c_ctx[None], c, jnp.zeros((SUBLANES - 1 - dec_batch, D_MODEL), F32)], axis=0)
    mods = _modulation(cvec, w_mod, b_mod).reshape(DEPTH, SUBLANES, 3, 3, D_MODEL)

    rope = _rope_tables(dec_seq)

    ffn_w = [w_ffn_in[0, 0].astype(BF16), w_ffn_out[0, 0].astype(BF16)]
    ffn_src = (w_ffn_in, w_ffn_out)

    def ffn_sublayer(n, mixers):
        i, s = divmod(n, 2)
        m, g = mods[i, :, 2 * s], norm_g[i, 2 * s]
        nxt = divmod(n + 1, 2) if n + 1 < 2 * DEPTH else None
        w_in, w_out = ffn_w
        for si, st in enumerate(streams):
            cast = (ffn_src[si],) + nxt if nxt is not None else None
            ys[si], converted = _ffn(st, ys[si], m, g, w_in, w_out, 0.5,
                                     mixer=None if mixers is None else mixers[si], cast=cast)
            if nxt is not None:
                ffn_w[si] = converted

    ys = [x_prompt.reshape(prompt.rows, D_MODEL), x_sample.reshape(sample.rows, D_MODEL)]
    new_k = new_v = new_s = None
    for i in range(DEPTH):
        kind, j = i % N_MIXERS, i // N_MIXERS
        g = norm_g[i]
        ffn_sublayer(2 * i, None)

        m1 = mods[i, :, 1]
        mixers = []
        for si, st in enumerate(streams):
            y = ys[si]
            is_prompt = si == 0
            bias_out = None
            if kind == 0:
                q, k, v = _proj(st, y, m1, g[1], w_attn_qkv[j].astype(BF16), None, "split", 3)
                if is_prompt:
                    new_k = k.reshape(batch, 1, seq, ATTN_HEADS, HEAD_W)
                    new_v = v.reshape(batch, 1, seq, ATTN_HEADS, HEAD_W)
                    mix = _attention(st, q, k, v, attn_lambda[j], attn_subln_g[j], i, ATTN_HEADS)
                else:
                    ctx = (cache_k.reshape(dec_batch, -1, D_MODEL), cache_v.reshape(dec_batch, -1, D_MODEL),
                           j, past, rope[0], rope[1])
                    mix = _attention(st, q, k, v, attn_lambda[j], attn_subln_g[j], i, ATTN_HEAD_GROUP, ctx)
                w_o = w_attn_o[j]
            elif kind == 1:
                parts = _proj(st, y, m1, g[1], w_hgrn_in[j].astype(BF16), None, "hgrn", 7,
                              lb=hgrn_lb, layer_idx=i)
                if is_prompt:
                    mix, s_fin = _hgrn(st, parts, hgrn_norm_g[j], emit_state=True)
                    new_s = s_fin[:, None]
                else:
                    mix, _ = _hgrn(st, parts, hgrn_norm_g[j], state=state_hgrn, state_layer=j)
                w_o = w_hgrn_o[j]
            elif kind == 2:
                (u,) = _proj(st, y, m1, g[1], w_cm_in[j].astype(BF16), b_cm_in[j], "glu", 1)
                mix = _conv_call(st, _conformer_conv_kernel, [u], [],
                                 [w_cm_dw[j], b_cm_dw[j].reshape(1, D_MODEL),
                                  cm_ln_g[j].reshape(1, D_MODEL), cm_ln_b[j].reshape(1, D_MODEL)])
                w_o = w_cm_out[j]
                bias_out = b_cm_out[j]
            else:
                bg, prod = _proj(st, y, m1, g[1], w_sc_in[j].astype(BF16), None, "gated_pair", 2)
                mix = _conv_call(st, _short_conv_kernel, [prod], [bg], [w_sc_conv[j]])
                w_o = w_sc_out[j]
            mixers.append((mix, w_o.astype(BF16), bias_out, m1, g[1]))

        ffn_sublayer(2 * i + 1, mixers)

    return (ys[0].reshape(batch, seq, D_MODEL), ys[1].reshape(dec_batch, dec_seq, D_MODEL),
            new_k, new_v, new_s)
```

```python
import functools
import math

import jax
import jax.numpy as jnp
from jax import lax
from jax.experimental import pallas as pl
from jax.experimental.pallas import tpu as pltpu

D_MODEL = 1024
DEPTH = 4
GRID_W = 64
N_MIXERS = 4
ATTN_HEAD_DIM = 64
ATTN_HEADS = D_MODEL // (2 * ATTN_HEAD_DIM)
HEAD_W = 2 * ATTN_HEAD_DIM
ROPE_BASE = 10000.0
HGRN_DK = 128
HGRN_HEADS = D_MODEL // HGRN_DK
HGRN_DV = D_MODEL // HGRN_HEADS
CONV_WIDTH = 31
SHORT_CONV_WIDTH = 3
D_FF = ((8 * D_MODEL // 3 + 127) // 128) * 128
EPS = 1e-6

F32 = jnp.float32
BF16 = jnp.bfloat16

SUBLANES = 8
FFN_TILE = 1024
TOKEN_TILE = 512
SUB_ROWS = 256
ATTN_Q_TILE = 512
ATTN_KEY_BLOCK = 512
ATTN_HEAD_GROUP = 2
GLA_CHUNK = 128
CONV_TILE = 256
CONV_HALO = 16
MOD_TILE = 2304
VMEM_LIMIT = 56 * 1024 * 1024
FFN_VMEM_LIMIT = 60 * 1024 * 1024
LOGF_MIN = -180.0
GLA_WINDOW_ROWS = 4096
GLA_MAX_HEADS = 4


def _bdot(a, b):
    return jnp.dot(a, b, preferred_element_type=F32)


def _bdot_nt(a, b):
    return lax.dot_general(a, b, (((1,), (1,)), ((), ())), preferred_element_type=F32)


def _rms(x, g):
    return x * lax.rsqrt(jnp.mean(x * x, axis=-1, keepdims=True) + EPS) * g


def _silu(x):
    return x * jax.nn.sigmoid(x)


def _split2(x):
    hi = x.astype(BF16)
    lo = (x - hi.astype(F32)).astype(BF16)
    return hi, lo


def _split3(x):
    hi = x.astype(BF16)
    r = x - hi.astype(F32)
    mid = r.astype(BF16)
    lo = (r - mid.astype(F32)).astype(BF16)
    return hi, mid, lo


def _params(sem, vmem=VMEM_LIMIT):
    return pltpu.CompilerParams(dimension_semantics=sem, vmem_limit_bytes=vmem)


def _interleave(gens):
    results = [None] * len(gens)
    active = list(enumerate(gens))
    while active:
        still = []
        for i, g in active:
            try:
                next(g)
                still.append((i, g))
            except StopIteration as done:
                results[i] = done.value
        active = still
    return results


def _resident(shape):
    return pl.BlockSpec(shape, lambda *_: (0,) * len(shape), pipeline_mode=pl.Buffered(1))


def _mod_kernel(c_ref, w_ref, b_ref, o_ref):
    a_hi, a_lo = _split2(_silu(c_ref[...]))
    w_hi, w_lo = _split2(w_ref[0])
    acc = _bdot(a_hi, w_hi) + _bdot(a_lo, w_hi) + _bdot(a_hi, w_lo)
    o_ref[0] = acc + b_ref[0]


def _modulation(cvec, w_mod, b_mod):
    rows = cvec.shape[0]
    n = w_mod.shape[-1]
    return pl.pallas_call(
        _mod_kernel,
        grid=(DEPTH, n // MOD_TILE),
        in_specs=[
            pl.BlockSpec((rows, D_MODEL), lambda i, j: (0, 0)),
            pl.BlockSpec((1, D_MODEL, MOD_TILE), lambda i, j: (i, 0, j)),
            pl.BlockSpec((1, 1, MOD_TILE), lambda i, j: (i, 0, j)),
        ],
        out_specs=pl.BlockSpec((1, rows, MOD_TILE), lambda i, j: (i, 0, j)),
        out_shape=jax.ShapeDtypeStruct((DEPTH, rows, n), F32),
        compiler_params=_params(("arbitrary", "arbitrary")),
        name="modulation",
    )(cvec, w_mod, b_mod.reshape(DEPTH, 1, n))


class _Stream:
    def __init__(self, batch, seq, mod_row0, per_batch_mod):
        self.batch = batch
        self.seq = seq
        self.rows = batch * seq
        self.mod_row0 = mod_row0
        self.tokens_per_mod = seq if per_batch_mod else batch * seq

    def mod_spec(self, tile):
        row0, tpm = self.mod_row0, self.tokens_per_mod
        assert tpm % tile == 0
        return pl.BlockSpec((1, 3, D_MODEL), lambda t, *_: (row0 + (t * tile) // tpm, 0, 0))


def _row_groups(ref):
    return [pl.ds(s, SUB_ROWS) for s in range(0, ref.shape[0], SUB_ROWS)]


def _ffn_kernel(*refs, weight, has_mixer, mixer_bias, has_cast):
    it = iter(refs)
    x_ref = next(it)
    if has_mixer:
        mix_ref, wmix_ref = next(it), next(it)
        bmix_ref = next(it) if mixer_bias else None
        mm_ref, gm_ref = next(it), next(it)
    m_ref, g_ref, wi_ref, wo_ref = next(it), next(it), next(it), next(it)
    cast_src = next(it) if has_cast else None
    o_ref = next(it)
    if has_cast:
        next(it)[...] = cast_src[...].astype(BF16)

    m = m_ref[0]
    shift, scale, gate_m = m[0:1, :], 1.0 + m[1:2, :], weight * m[2:3, :]
    def residual_in(rows):
        x = x_ref[rows, :]
        if has_mixer:
            mixed = _bdot(mix_ref[rows, :], wmix_ref[...])
            if mixer_bias:
                mixed = mixed + bmix_ref[...]
            x = x + mm_ref[0][2:3, :] * _rms(mixed, gm_ref[1:2, :])
        return x

    groups = _row_groups(x_ref)
    x_next = residual_in(groups[0])
    for n, rows in enumerate(groups):
        x = x_next
        if n + 1 < len(groups):
            x_next = residual_in(groups[n + 1])
        h = (_rms(x, g_ref[0:1, :]) * scale + shift).astype(BF16)
        gate = _bdot(h, wi_ref[:, :D_FF])
        up = _bdot(h, wi_ref[:, D_FF:])
        act = (_silu(gate) * up).astype(BF16)
        out = _bdot(act, wo_ref[...])
        o_ref[rows, :] = x + gate_m * _rms(out, g_ref[1:2, :])


def _ffn(stream, y, m, g, w_in, w_out, weight, mixer=None, cast=None):
    tm = FFN_TILE
    steps = stream.rows // tm
    in_specs = [pl.BlockSpec((tm, D_MODEL), lambda t: (t, 0))]
    args = [y]
    mixer_bias = False
    if mixer is not None:
        mix, w_mix, b_mix, m_mix, g_mix = mixer
        mixer_bias = b_mix is not None
        in_specs += [pl.BlockSpec((tm, D_MODEL), lambda t: (t, 0)), _resident((D_MODEL, D_MODEL))]
        args += [mix, w_mix]
        if mixer_bias:
            in_specs.append(_resident((1, D_MODEL)))
            args.append(b_mix.reshape(1, D_MODEL))
        in_specs += [stream.mod_spec(tm), _resident((2, D_MODEL))]
        args += [m_mix, g_mix]
    in_specs += [stream.mod_spec(tm), _resident((2, D_MODEL)),
                 _resident((D_MODEL, 2 * D_FF)), _resident((D_FF, D_MODEL))]
    args += [m, g, w_in, w_out]
    out_specs = [pl.BlockSpec((tm, D_MODEL), lambda t: (t, 0))]
    out_shape = [jax.ShapeDtypeStruct((stream.rows, D_MODEL), F32)]
    if cast is not None:
        src, layer, which = cast
        r, c = src.shape[2:]
        in_specs.append(pl.BlockSpec((None, None, r // steps, c), lambda t: (layer, which, t, 0)))
        args.append(src)
        out_specs.append(pl.BlockSpec((r // steps, c), lambda t: (t, 0)))
        out_shape.append(jax.ShapeDtypeStruct((r, c), BF16))
    res = pl.pallas_call(
        functools.partial(_ffn_kernel, weight=weight, has_mixer=mixer is not None,
                          mixer_bias=mixer_bias, has_cast=cast is not None),
        grid=(steps,),
        in_specs=in_specs,
        out_specs=out_specs,
        out_shape=out_shape,
        compiler_params=_params(("parallel",), FFN_VMEM_LIMIT),
        name="ffn",
    )(*args)
    return (res[0], res[1]) if cast is not None else (res[0], None)


def _gla_gates(z, lbd):
    e = jnp.exp(-jnp.abs(z))
    r = 1.0 / (1.0 + e)
    pos = z >= 0.0
    sig = jnp.where(pos, r, e * r)
    sig_neg = jnp.where(pos, e * r, r)
    logf = jnp.maximum(jnp.log(lbd + (1.0 - lbd) * sig), LOGF_MIN)
    return logf, (1.0 - lbd) * sig_neg


def _proj_kernel(*refs, mode, has_bias, layer_idx):
    x_ref, m_ref, g_ref, w_ref = refs[:4]
    refs = refs[4:]
    b_ref = lb_ref = None
    if has_bias:
        b_ref, refs = refs[0], refs[1:]
    if mode == "hgrn":
        lb_ref, refs = refs[0], refs[1:]
    outs = refs
    m = m_ref[0]
    shift, scale = m[0:1, :], 1.0 + m[1:2, :]

    lower = []
    if mode == "hgrn":
        for d in range(2):
            lbp = lb_ref[d]
            e = jnp.exp(lbp - jnp.max(lbp, axis=0, keepdims=True))
            p = e / jnp.sum(e, axis=0, keepdims=True)
            lower.append(jnp.sum(p[1:layer_idx + 1, :], axis=0, keepdims=True))

    for rows in _row_groups(x_ref):
        h = (_rms(x_ref[rows, :], g_ref[0:1, :]) * scale + shift).astype(BF16)

        def part(p):
            r = _bdot(h, w_ref[:, p * D_MODEL:(p + 1) * D_MODEL])
            if has_bias:
                r = r + b_ref[:, p * D_MODEL:(p + 1) * D_MODEL]
            return r

        if mode == "split":
            for p, o_ref in enumerate(outs):
                o_ref[rows, :] = part(p)
        elif mode == "glu":
            a = part(0)
            outs[0][rows, :] = a * jax.nn.sigmoid(part(1))
        elif mode == "gated_pair":
            outs[0][rows, :] = part(0)
            outs[1][rows, :] = part(1) * part(2)
        else:
            outs[0][rows, :] = part(0)
            for d in range(2):
                logf, k = _gla_gates(part(1 + d), lower[d])
                outs[1 + 2 * d][rows, :] = logf
                outs[2 + 2 * d][rows, :] = k
            outs[5][rows, :] = part(3)
            outs[6][rows, :] = part(4)


def _proj(stream, y, m, g, w, b, mode, n_out, lb=None, layer_idx=0):
    tm = TOKEN_TILE
    n = w.shape[1]
    has_bias = b is not None
    in_specs = [
        pl.BlockSpec((tm, D_MODEL), lambda t: (t, 0)),
        stream.mod_spec(tm),
        _resident((2, D_MODEL)),
        _resident((D_MODEL, n)),
    ]
    args = [y, m, g, w]
    if has_bias:
        in_specs.append(_resident((1, n)))
        args.append(b.reshape(1, n))
    if mode == "hgrn":
        in_specs.append(_resident(lb.shape))
        args.append(lb)
    return pl.pallas_call(
        functools.partial(_proj_kernel, mode=mode, has_bias=has_bias, layer_idx=layer_idx),
        grid=(stream.rows // tm,),
        in_specs=in_specs,
        out_specs=[pl.BlockSpec((tm, D_MODEL), lambda t: (t, 0))] * n_out,
        out_shape=[jax.ShapeDtypeStruct((stream.rows, D_MODEL), F32)] * n_out,
        compiler_params=_params(("parallel",)),
        name="proj_" + mode,
    )(*args)


def _reduce_rows(x, op):
    r = x.shape[0]
    while r > SUBLANES:
        fold = next((f for f in (4, 2) if r % (f * SUBLANES) == 0), None)
        if fold is None:
            break
        x = op(x.reshape(fold, r // fold, x.shape[1]), axis=0)
        r //= fold
    return op(x, axis=0, keepdims=True)


def _rope(x, cos, sin_signed):
    lane = lax.broadcasted_iota(jnp.int32, x.shape, 1)
    first = (lane % (ATTN_HEAD_DIM // 2)) < (ATTN_HEAD_DIM // 4)
    rot = jnp.where(first, pltpu.roll(x, HEAD_W - ATTN_HEAD_DIM // 4, 1),
                    pltpu.roll(x, ATTN_HEAD_DIM // 4, 1))
    return x * cos + rot * sin_signed


def _attn_kernel(*refs, has_ctx, lam_init, seq, past, heads):
    if has_ctx:
        (q_ref, k_ref, v_ref, ck_ref, cv_ref, cosq_ref, sinq_ref, cosk_ref, sink_ref,
         lam_ref, g_ref, o_ref, k_scr, vt_scr) = refs
    else:
        q_ref, k_ref, v_ref, lam_ref, g_ref, o_ref, k_scr, vt_scr = refs

    def head_cols(h):
        return slice(h * HEAD_W, (h + 1) * HEAD_W)

    @pl.when(pl.program_id(2) == 0)
    def _():
        for h in range(heads):
            cols = head_cols(h)
            if has_ctx:
                k_scr[h, 0:past, :] = ck_ref[0, :, cols].astype(BF16)
                vt_scr[h, :, 0:past] = cv_ref[0, :, cols].T.astype(BF16)
                k_scr[h, past:past + seq, :] = _rope(k_ref[:, cols], cosk_ref[...], sink_ref[...]).astype(BF16)
                vt_scr[h, :, past:past + seq] = v_ref[:, cols].T.astype(BF16)
            else:
                k_scr[h] = k_ref[:, cols].astype(BF16)
                vt_scr[h] = v_ref[:, cols].T.astype(BF16)

    lp = lam_ref[...]
    lam = (jnp.exp(jnp.sum(lp[0:1, :] * lp[1:2, :], axis=-1, keepdims=True))
           - jnp.exp(jnp.sum(lp[2:3, :] * lp[3:4, :], axis=-1, keepdims=True)) + lam_init)

    n_keys = past + seq
    key_block = min(ATTN_KEY_BLOCK, n_keys)

    def softmax_pv(h, qm):
        def scores(j):
            return _bdot_nt(k_scr[h, j:j + key_block, :], qm)

        m = l = acc = None
        s_next = scores(0)
        for j in range(0, n_keys, key_block):
            s = s_next
            if j + key_block < n_keys:
                s_next = scores(j + key_block)
            yield
            mj = _reduce_rows(s, jnp.max)
            m_new = mj if m is None else jnp.maximum(m, mj)
            p = jnp.exp(s - m_new)
            lj = _reduce_rows(p, jnp.sum)
            yield
            pv = _bdot(vt_scr[h, :, j:j + key_block], p.astype(BF16))
            if m is None:
                l, acc = lj, pv
            else:
                alpha = jnp.exp(m - m_new)
                l, acc = alpha * l + lj, alpha * acc + pv
            m = m_new
            yield
        return acc / l

    for h0 in range(0, heads, ATTN_HEAD_GROUP):
        group = range(h0, min(h0 + ATTN_HEAD_GROUP, heads))
        branches = []
        for h in group:
            q = q_ref[:, head_cols(h)]
            if has_ctx:
                q = _rope(q, cosq_ref[...], sinq_ref[...])
            q = q * (ATTN_HEAD_DIM ** -0.5)
            lane = lax.broadcasted_iota(jnp.int32, q.shape, 1)
            branches.append(softmax_pv(h, jnp.where(lane < ATTN_HEAD_DIM, q, 0.0).astype(BF16)))
            branches.append(softmax_pv(h, jnp.where(lane >= ATTN_HEAD_DIM, q, 0.0).astype(BF16)))
        outs = _interleave(branches)
        for n, h in enumerate(group):
            ot = outs[2 * n] - lam * outs[2 * n + 1]
            o_ref[:, head_cols(h)] = (_rms(ot.T, g_ref[...]) * (1.0 - lam_init)).astype(BF16)


def _attention(stream, q, k, v, lam_params, subln_g, layer_idx, heads, ctx=None):
    seq, tq = stream.seq, min(ATTN_Q_TILE, stream.seq)
    nq = seq // tq
    width = heads * HEAD_W
    lam_init = 0.8 - 0.6 * math.exp(-0.3 * layer_idx)
    has_ctx = ctx is not None
    in_specs = [
        pl.BlockSpec((tq, width), lambda b, h, i: (b * nq + i, h)),
        pl.BlockSpec((seq, width), lambda b, h, i: (b, h)),
        pl.BlockSpec((seq, width), lambda b, h, i: (b, h)),
    ]
    args = [q, k, v]
    past = 0
    if has_ctx:
        ck, cv, cache_layer, past, cos, sin_signed = ctx
        in_specs += [
            pl.BlockSpec((1, past, width), lambda b, h, i: (b, cache_layer, h)),
            pl.BlockSpec((1, past, width), lambda b, h, i: (b, cache_layer, h)),
            pl.BlockSpec((tq, HEAD_W), lambda b, h, i: (i, 0)),
            pl.BlockSpec((tq, HEAD_W), lambda b, h, i: (i, 0)),
            _resident((seq, HEAD_W)),
            _resident((seq, HEAD_W)),
        ]
        args += [ck, cv, cos, sin_signed, cos, sin_signed]
    in_specs += [_resident((4, ATTN_HEAD_DIM)), _resident((1, HEAD_W))]
    args += [lam_params, subln_g.reshape(1, HEAD_W)]
    return pl.pallas_call(
        functools.partial(_attn_kernel, has_ctx=has_ctx, lam_init=lam_init, seq=seq, past=past, heads=heads),
        grid=(stream.batch, ATTN_HEADS // heads, nq),
        in_specs=in_specs,
        out_specs=pl.BlockSpec((tq, width), lambda b, h, i: (b * nq + i, h)),
        out_shape=jax.ShapeDtypeStruct((stream.rows, D_MODEL), BF16),
        scratch_shapes=[pltpu.VMEM((heads, past + seq, HEAD_W), BF16),
                        pltpu.VMEM((heads, HEAD_W, past + seq), BF16)],
        compiler_params=_params(("parallel", "parallel", "arbitrary")),
        name="diff_attention",
    )(*args)


def _rope_tables(n_tokens):
    rows = n_tokens // GRID_W
    row = jnp.repeat(jnp.arange(rows), GRID_W).astype(F32)
    col = jnp.tile(jnp.arange(GRID_W), rows).astype(F32)
    n_freq = ATTN_HEAD_DIM // 4
    inv_freq = ROPE_BASE ** (-jnp.arange(n_freq, dtype=F32) / n_freq)
    ang_r = row[:, None] * inv_freq
    ang_c = col[:, None] * inv_freq
    ang = jnp.concatenate([ang_r, ang_r, ang_c, ang_c], axis=-1)
    sign = jnp.concatenate([-jnp.ones((n_freq,), F32), jnp.ones((n_freq,), F32)] * 2)
    cos = jnp.cos(ang)
    sin_signed = jnp.sin(ang) * sign
    return jnp.tile(cos, (1, 2)), jnp.tile(sin_signed, (1, 2))


def _boundary_rows(b, m, fwd):
    c = b.shape[0]
    idx = m - 1 if fwd else m
    if 2 * m >= SUBLANES:
        b3 = b.reshape(c // (2 * m), 2 * m, HGRN_DK)
        return jnp.broadcast_to(b3[:, idx:idx + 1, :], b3.shape).reshape(c, HGRN_DK)
    b3 = b.reshape(c // SUBLANES, SUBLANES, HGRN_DK)
    trow = lax.broadcasted_iota(jnp.int32, b3.shape, 1)
    br = jnp.broadcast_to(b3[:, idx:idx + 1, :], b3.shape)
    for start in range(2 * m, SUBLANES, 2 * m):
        br = jnp.where(trow >= start, b3[:, start + idx:start + idx + 1, :], br)
    return br.reshape(c, HGRN_DK)


def _gla_chunk(q, k, v, logf, st, tri, level, sides, fwd):
    c = q.shape[0]
    hi, mid, lo = _split3(logf)
    yield
    b = _bdot(tri, hi) + _bdot(tri, mid) + _bdot(tri, lo)
    yield
    b_last = b[c - 1:c, :] if fwd else b[0:1, :]
    qe = (q * jnp.exp(b)).astype(BF16)
    ke = (k * jnp.exp(b_last - b)).astype(BF16)
    yield
    o = _bdot_nt(qe, st.astype(BF16))
    st_new = st * jnp.exp(b_last) + _bdot(v.T.astype(BF16), ke)
    yield

    sc = jnp.where(level == -1, jnp.sum(q * k, axis=-1, keepdims=True), 0.0)
    pending = None
    for lg, side in enumerate(sides):
        ex = jnp.exp((b - _boundary_rows(b, 2 ** lg, fwd)) * side)
        z = (jnp.where(side > 0.0, q, k) * ex).astype(BF16)
        yield
        if pending is not None:
            sc = jnp.where(level == pending[0], pending[1], sc)
        pending = (lg, _bdot_nt(z, z))
    yield
    sc = jnp.where(level == pending[0], pending[1], sc)
    return o + _bdot(sc.astype(BF16), v.astype(BF16)), st_new


def _hgrn_kernel(*refs, has_state, emit_state, seq, heads):
    q_ref, lff_ref, kf_ref, lfb_ref, kb_ref, i_ref, g_ref, ng_ref = refs[:8]
    rest = list(refs[8:])
    s0_refs = [rest.pop(0), rest.pop(0)] if has_state else [None, None]
    o_ref = rest.pop(0)
    sfin_ref = rest.pop(0) if emit_state else None
    of_scr, ob_scr, stf_scr, stb_scr = rest
    c = GLA_CHUNK
    n_chunks = seq // c

    ti = lax.broadcasted_iota(jnp.int32, (c, c), 0)
    si = lax.broadcasted_iota(jnp.int32, (c, c), 1)
    parted = 31 - lax.clz(ti ^ si)
    row = lax.broadcasted_iota(jnp.int32, (c, HGRN_DK), 0)
    later = [jnp.where((row & (1 << lg)) != 0, 1.0, -1.0) for lg in range(int(math.log2(c)))]
    dirs = (
        (True, lff_ref, kf_ref, stf_scr, of_scr, jnp.where(si <= ti, 1.0, 0.0).astype(BF16),
         jnp.where(si <= ti, parted, -2), later),
        (False, lfb_ref, kb_ref, stb_scr, ob_scr, jnp.where(si >= ti, 1.0, 0.0).astype(BF16),
         jnp.where(si >= ti, parted, -2), [-x for x in later]),
    )
    for h in range(heads):
        for d in range(2):
            dirs[d][3][h] = s0_refs[d][h].T if has_state else jnp.zeros((HGRN_DV, HGRN_DK), F32)

    def body(i, carry):
        chains, sinks = [], []
        for h in range(heads):
            cols = slice(h * HGRN_DK, (h + 1) * HGRN_DK)
            for fwd, lf_ref, k_ref, st_scr, o_scr, tri, level, sides in dirs:
                ci = i if fwd else n_chunks - 1 - i
                rows = pl.ds(pl.multiple_of(ci * c, c), c)
                chains.append(_gla_chunk(q_ref[rows, cols], k_ref[rows, cols], i_ref[rows, cols],
                                         lf_ref[rows, cols], st_scr[h], tri, level, sides, fwd))
                sinks.append((st_scr, o_scr, h, rows, cols))
        for (o, st_new), (st_scr, o_scr, h, rows, cols) in zip(_interleave(chains), sinks):
            st_scr[h] = st_new
            o_scr[rows, cols] = o
        return carry

    lax.fori_loop(0, n_chunks, body, 0)
    for h in range(heads):
        cols = slice(h * HGRN_DK, (h + 1) * HGRN_DK)
        if emit_state:
            sfin_ref[0, h] = stf_scr[h].T
            sfin_ref[1, h] = stb_scr[h].T
        o_ref[:, cols] = (_rms(of_scr[:, cols] + ob_scr[:, cols], ng_ref[...])
                          * _silu(g_ref[:, cols])).astype(BF16)


def _hgrn(stream, parts, norm_g, state=None, state_layer=0, emit_state=False):
    seq = stream.seq
    heads = max(1, min(GLA_MAX_HEADS, GLA_WINDOW_ROWS // seq))
    width = heads * HGRN_DK
    has_state = state is not None
    col = pl.BlockSpec((seq, width), lambda b, h: (b, h))
    in_specs = [col] * 7 + [_resident((1, HGRN_DV))]
    args = list(parts) + [norm_g.reshape(1, HGRN_DV)]
    if has_state:
        for d in range(2):
            in_specs.append(pl.BlockSpec((None, None, None, heads, HGRN_DK, HGRN_DV),
                                         lambda b, h, d=d: (b, state_layer, d, h, 0, 0)))
            args.append(state)
    out_specs = [pl.BlockSpec((seq, width), lambda b, h: (b, h))]
    out_shape = [jax.ShapeDtypeStruct((stream.rows, D_MODEL), BF16)]
    if emit_state:
        out_specs.append(pl.BlockSpec((None, 2, heads, HGRN_DK, HGRN_DV), lambda b, h: (b, 0, h, 0, 0)))
        out_shape.append(jax.ShapeDtypeStruct((stream.batch, 2, HGRN_HEADS, HGRN_DK, HGRN_DV), F32))
    res = pl.pallas_call(
        functools.partial(_hgrn_kernel, has_state=has_state, emit_state=emit_state, seq=seq, heads=heads),
        grid=(stream.batch, HGRN_HEADS // heads),
        in_specs=in_specs,
        out_specs=out_specs,
        out_shape=out_shape,
        scratch_shapes=[pltpu.VMEM((seq, width), F32), pltpu.VMEM((seq, width), F32),
                        pltpu.VMEM((heads, HGRN_DV, HGRN_DK), F32),
                        pltpu.VMEM((heads, HGRN_DV, HGRN_DK), F32)],
        compiler_params=_params(("parallel", "parallel")),
        name="hgrn2",
    )(*args)
    return res if emit_state else (res[0], None)


def _fill_padded(pad_scr, src_ref, seq):
    zeros = jnp.zeros((CONV_HALO, D_MODEL), F32)
    pad_scr[0:CONV_HALO, :] = zeros
    pad_scr[CONV_HALO:CONV_HALO + seq, :] = src_ref[...]
    pad_scr[CONV_HALO + seq:2 * CONV_HALO + seq, :] = zeros


def _dwconv_tile(pad_scr, w_ref, width):
    tile = CONV_TILE
    base = pl.program_id(1) * tile
    first = CONV_HALO - width // 2
    acc = None
    for r in range(SUBLANES):
        part = None
        for j in range(width):
            if (first + j) % SUBLANES != r:
                continue
            start = pl.multiple_of(base + (first + j - r), SUBLANES)
            term = pad_scr[pl.ds(start, tile + SUBLANES), :] * w_ref[j:j + 1, :]
            part = term if part is None else part + term
        if part is not None:
            shifted = part[r:r + tile, :]
            acc = shifted if acc is None else acc + shifted
    return acc


def _conformer_conv_kernel(u_ref, w_ref, bdw_ref, lng_ref, lnb_ref, o_ref, pad_scr, *, seq):
    @pl.when(pl.program_id(1) == 0)
    def _():
        _fill_padded(pad_scr, u_ref, seq)

    u = _dwconv_tile(pad_scr, w_ref, CONV_WIDTH) + bdw_ref[...]
    xc = u - jnp.mean(u, axis=-1, keepdims=True)
    y = xc * lax.rsqrt(jnp.mean(xc * xc, axis=-1, keepdims=True) + EPS) * lng_ref[...] + lnb_ref[...]
    o_ref[...] = _silu(y).astype(BF16)


def _short_conv_kernel(p_ref, bg_ref, w_ref, o_ref, pad_scr, *, seq):
    @pl.when(pl.program_id(1) == 0)
    def _():
        _fill_padded(pad_scr, p_ref, seq)

    o_ref[...] = (bg_ref[...] * _dwconv_tile(pad_scr, w_ref, SHORT_CONV_WIDTH)).astype(BF16)


def _conv_call(stream, body, seq_inputs, tile_inputs, small_inputs):
    seq, tile = stream.seq, CONV_TILE
    nt = seq // tile
    in_specs = ([pl.BlockSpec((seq, D_MODEL), lambda b, t: (b, 0))] * len(seq_inputs)
                + [pl.BlockSpec((tile, D_MODEL), lambda b, t: (b * nt + t, 0))] * len(tile_inputs)
                + [_resident(a.shape) for a in small_inputs])
    return pl.pallas_call(
        functools.partial(body, seq=seq),
        grid=(stream.batch, nt),
        in_specs=in_specs,
        out_specs=pl.BlockSpec((tile, D_MODEL), lambda b, t: (b * nt + t, 0)),
        out_shape=jax.ShapeDtypeStruct((stream.rows, D_MODEL), BF16),
        scratch_shapes=[pltpu.VMEM((seq + 2 * CONV_HALO, D_MODEL), F32)],
        compiler_params=_params(("parallel", "arbitrary")),
        name="dwconv",
    )(*seq_inputs, *tile_inputs, *small_inputs)


def kernel(x_prompt, x_sample, c, cache_k, cache_v, state_hgrn, c_ctx, w_mod, b_mod, norm_g,
           w_ffn_in, w_ffn_out, w_attn_qkv, w_attn_o, attn_lambda, attn_subln_g,
           w_hgrn_in, w_hgrn_o, hgrn_norm_g, hgrn_lb, w_cm_in, b_cm_in, w_cm_dw, b_cm_dw,
           cm_ln_g, cm_ln_b, w_cm_out, b_cm_out, w_sc_in, w_sc_conv, w_sc_out):
    batch, seq, _ = x_prompt.shape
    dec_batch, dec_seq, _ = x_sample.shape
    past = cache_k.shape[2]
    prompt = _Stream(batch, seq, 0, per_batch_mod=False)
    sample = _Stream(dec_batch, dec_seq, 1, per_batch_mod=True)
    streams = (prompt, sample)

    cvec = jnp.concatenate([c_ctx[None], c, jnp.zeros((SUBLANES - 1 - dec_batch, D_MODEL), F32)], axis=0)
    mods = _modulation(cvec, w_mod, b_mod).reshape(DEPTH, SUBLANES, 3, 3, D_MODEL)

    rope = _rope_tables(dec_seq)

    ffn_w = [w_ffn_in[0, 0].astype(BF16), w_ffn_out[0, 0].astype(BF16)]
    ffn_src = (w_ffn_in, w_ffn_out)

    def ffn_sublayer(n, mixers):
        i, s = divmod(n, 2)
        m, g = mods[i, :, 2 * s], norm_g[i, 2 * s]
        nxt = divmod(n + 1, 2) if n + 1 < 2 * DEPTH else None
        w_in, w_out = ffn_w
        for si, st in enumerate(streams):
            cast = (ffn_src[si],) + nxt if nxt is not None else None
            ys[si], converted = _ffn(st, ys[si], m, g, w_in, w_out, 0.5,
                                     mixer=None if mixers is None else mixers[si], cast=cast)
            if nxt is not None:
                ffn_w[si] = converted

    ys = [x_prompt.reshape(prompt.rows, D_MODEL), x_sample.reshape(sample.rows, D_MODEL)]
    new_k = new_v = new_s = None
    for i in range(DEPTH):
        kind, j = i % N_MIXERS, i // N_MIXERS
        g = norm_g[i]
        ffn_sublayer(2 * i, None)

        m1 = mods[i, :, 1]
        mixers = []
        for si, st in enumerate(streams):
            y = ys[si]
            is_prompt = si == 0
            bias_out = None
            if kind == 0:
                q, k, v = _proj(st, y, m1, g[1], w_attn_qkv[j].astype(BF16), None, "split", 3)
                if is_prompt:
                    new_k = k.reshape(batch, 1, seq, ATTN_HEADS, HEAD_W)
                    new_v = v.reshape(batch, 1, seq, ATTN_HEADS, HEAD_W)
                    mix = _attention(st, q, k, v, attn_lambda[j], attn_subln_g[j], i, ATTN_HEADS)
                else:
                    ctx = (cache_k.reshape(dec_batch, -1, D_MODEL), cache_v.reshape(dec_batch, -1, D_MODEL),
                           j, past, rope[0], rope[1])
                    mix = _attention(st, q, k, v, attn_lambda[j], attn_subln_g[j], i, ATTN_HEAD_GROUP, ctx)
                w_o = w_attn_o[j]
            elif kind == 1:
                parts = _proj(st, y, m1, g[1], w_hgrn_in[j].astype(BF16), None, "hgrn", 7,
                              lb=hgrn_lb, layer_idx=i)
                if is_prompt:
                    mix, s_fin = _hgrn(st, parts, hgrn_norm_g[j], emit_state=True)
                    new_s = s_fin[:, None]
                else:
                    mix, _ = _hgrn(st, parts, hgrn_norm_g[j], state=state_hgrn, state_layer=j)
                w_o = w_hgrn_o[j]
            elif kind == 2:
                (u,) = _proj(st, y, m1, g[1], w_cm_in[j].astype(BF16), b_cm_in[j], "glu", 1)
                mix = _conv_call(st, _conformer_conv_kernel, [u], [],
                                 [w_cm_dw[j], b_cm_dw[j].reshape(1, D_MODEL),
                                  cm_ln_g[j].reshape(1, D_MODEL), cm_ln_b[j].reshape(1, D_MODEL)])
                w_o = w_cm_out[j]
                bias_out = b_cm_out[j]
            else:
                bg, prod = _proj(st, y, m1, g[1], w_sc_in[j].astype(BF16), None, "gated_pair", 2)
                mix = _conv_call(st, _short_conv_kernel, [prod], [bg], [w_sc_conv[j]])
                w_o = w_sc_out[j]
            mixers.append((mix, w_o.astype(BF16), bias_out, m1, g[1]))

        ffn_sublayer(2 * i + 1, mixers)

    return (ys[0].reshape(batch, seq, D_MODEL), ys[1].reshape(dec_batch, dec_seq, D_MODEL),
            new_k, new_v, new_s)
```
